```python
import math
import jax, jax.numpy as jnp
from jax import lax
import numpy as np

D_MODEL = 1024
BATCH = 16
SEQ = 4096
DEPTH = 1

CHUNK = 64
Q_BLOCK = 128
ATT_HEADS = 8
ATT_HEAD_DIM = 64
ATT_V_DIM = 2 * ATT_HEAD_DIM
ATT_QK_WIDTH = 2 * ATT_HEADS * ATT_HEAD_DIM
ATT_V_WIDTH = ATT_HEADS * ATT_V_DIM
SSM_GROUP = 16
SSM_WIDTH = D_MODEL
SSM_GROUPS = SSM_WIDTH // SSM_GROUP
SSM_STATE = 64
DT_MIN = 1e-3
DT_MAX = 1e-1
D_FF = 2816
CONV_W = 3
IN_WIDTH = SSM_WIDTH + 2 * ATT_QK_WIDTH + ATT_V_WIDTH + 2 * D_MODEL
DEEPNORM_ALPHA = (2.0 * DEPTH) ** 0.25
DEEPNORM_BETA = (8.0 * DEPTH) ** -0.25
LN_EPS = 1e-5
RMS_EPS = 1e-5

kernel_name = "hybrid_s5_diffattn_convglu_deepnorm_adaln"


def _layernorm(x):
    xf = x.astype(jnp.float32)
    mu = jnp.mean(xf, axis=-1, keepdims=True)
    var = jnp.mean(jnp.square(xf - mu), axis=-1, keepdims=True)
    return ((xf - mu) * lax.rsqrt(var + LN_EPS)).astype(x.dtype)


def _cmul(ar, ai, br, bi):
    return ar * br - ai * bi, ar * bi + ai * br


def _s5_branch(u, lam_re, lam_im, log_dt, b_re, b_im, c_re, c_im, d_skip, w_glu, b_glu):
    bsz, seq, _ = u.shape
    ug = u.reshape(bsz, seq, SSM_GROUPS, SSM_GROUP)
    dt = jnp.exp(log_dt)[:, None]
    mag = jnp.exp(lam_re * dt)
    ab_re = mag * jnp.cos(lam_im * dt)
    ab_im = mag * jnp.sin(lam_im * dt)
    den = lam_re * lam_re + lam_im * lam_im
    nr = ab_re - 1.0
    ni = ab_im
    coef_re = (nr * lam_re + ni * lam_im) / den
    coef_im = (ni * lam_re - nr * lam_im) / den
    bb_re, bb_im = _cmul(coef_re[..., None], coef_im[..., None], b_re, b_im)
    bu_re = jnp.einsum('bsgc,gpc->bsgp', ug, bb_re)
    bu_im = jnp.einsum('bsgc,gpc->bsgp', ug, bb_im)
    a_re = jnp.broadcast_to(ab_re, (1, seq) + ab_re.shape)
    a_im = jnp.broadcast_to(ab_im, (1, seq) + ab_im.shape)

    def combine(left, right):
        a1r, a1i, b1r, b1i = left
        a2r, a2i, b2r, b2i = right
        ar, ai = _cmul(a2r, a2i, a1r, a1i)
        br, bi = _cmul(a2r, a2i, b1r, b1i)
        return ar, ai, br + b2r, bi + b2i

    _, _, s_re, s_im = lax.associative_scan(combine, (a_re, a_im, bu_re, bu_im), axis=1)
    y = jnp.einsum('bsgp,gcp->bsgc', s_re, c_re) - jnp.einsum('bsgp,gcp->bsgc', s_im, c_im)
    y = y.reshape(bsz, seq, SSM_WIDTH) + d_skip * u
    z = jax.nn.gelu(y)
    return z * jax.nn.sigmoid(z @ w_glu + b_glu)


def _diff_attention(q, k, v, lam, subln_g, lambda_init):
    bsz, seq = q.shape[0], q.shape[1]
    nb = seq // Q_BLOCK
    k1 = k[:, :, :, 0, :]
    k2 = k[:, :, :, 1, :]
    qb = q.reshape(bsz, nb, Q_BLOCK, ATT_HEADS, 2, ATT_HEAD_DIM).transpose(1, 0, 2, 3, 4, 5)
    key_chunk = jnp.arange(seq) // CHUNK
    scale = ATT_HEAD_DIM ** -0.5

    def block(args):
        qi, i = args
        q_chunk = (i * Q_BLOCK + jnp.arange(Q_BLOCK)) // CHUNK
        mask = key_chunk[None, :] <= q_chunk[:, None]
        s1 = jnp.einsum('bqhd,bkhd->bhqk', qi[:, :, :, 0, :], k1).astype(jnp.float32) * scale
        s2 = jnp.einsum('bqhd,bkhd->bhqk', qi[:, :, :, 1, :], k2).astype(jnp.float32) * scale
        p1 = jax.nn.softmax(jnp.where(mask, s1, -jnp.inf), axis=-1)
        p2 = jax.nn.softmax(jnp.where(mask, s2, -jnp.inf), axis=-1)
        w = (p1 - lam * p2).astype(v.dtype)
        return jnp.einsum('bhqk,bkhe->bqhe', w, v)

    o = lax.map(block, (qb, jnp.arange(nb)))
    o = o.transpose(1, 0, 2, 3, 4).reshape(bsz, seq, ATT_HEADS, ATT_V_DIM)
    of = o.astype(jnp.float32)
    of = of * lax.rsqrt(jnp.mean(of * of, axis=-1, keepdims=True) + RMS_EPS)
    o = (of.astype(o.dtype) * subln_g) * (1.0 - lambda_init)
    return o.reshape(bsz, seq, ATT_V_WIDTH)


def setup_inputs(seed: int = 0) -> dict:
    key = jax.random.key(seed)
    ks = jax.random.split(key, 40)
    f32 = jnp.float32
    L, D, G, P = DEPTH, D_MODEL, SSM_GROUPS, SSM_STATE

    def nrm(k, shape, std):
        return std * jax.random.normal(k, shape, f32)

    n = jnp.arange(P, dtype=f32)
    return {
        "x": nrm(ks[0], (BATCH, SEQ, D), 1.0),
        "c": nrm(ks[1], (BATCH, D), 1.0),
        "w_mod": nrm(ks[2], (L, D, 6 * D), 0.5 * D ** -0.5),
        "b_mod": nrm(ks[3], (L, 6 * D), 0.02),
        "w_in": nrm(ks[4], (L, D, IN_WIDTH), D ** -0.5),
        "ssm_lambda_re": -0.5 + nrm(ks[5], (L, G, P), 0.01),
        "ssm_lambda_im": math.pi * n + nrm(ks[6], (L, G, P), 0.01),
        "ssm_log_dt": jax.random.uniform(ks[7], (L, G), f32, math.log(DT_MIN), math.log(DT_MAX)),
        "ssm_b_re": nrm(ks[8], (L, G, P, SSM_GROUP), (2.0 * SSM_GROUP) ** -0.5),
        "ssm_b_im": nrm(ks[9], (L, G, P, SSM_GROUP), (2.0 * SSM_GROUP) ** -0.5),
        "ssm_c_re": nrm(ks[10], (L, G, SSM_GROUP, P), (2.0 * P) ** -0.5),
        "ssm_c_im": nrm(ks[11], (L, G, SSM_GROUP, P), (2.0 * P) ** -0.5),
        "ssm_d": nrm(ks[12], (L, SSM_WIDTH), 1.0),
        "ssm_w_glu": nrm(ks[13], (L, SSM_WIDTH, SSM_WIDTH), SSM_WIDTH ** -0.5),
        "ssm_b_glu": nrm(ks[14], (L, SSM_WIDTH), 0.02),
        "att_lambda_q1": nrm(ks[15], (L, ATT_HEAD_DIM), 0.1),
        "att_lambda_k1": nrm(ks[16], (L, ATT_HEAD_DIM), 0.1),
        "att_lambda_q2": nrm(ks[17], (L, ATT_HEAD_DIM), 0.1),
        "att_lambda_k2": nrm(ks[18], (L, ATT_HEAD_DIM), 0.1),
        "att_subln_g": 1.0 + nrm(ks[19], (L, ATT_V_DIM), 0.02),
        "w_branch_ssm": nrm(ks[20], (L, SSM_WIDTH, D), SSM_WIDTH ** -0.5),
        "w_branch_att": nrm(ks[21], (L, ATT_V_WIDTH, D), ATT_V_WIDTH ** -0.5),
        "w_out": nrm(ks[22], (L, D, D), DEEPNORM_BETA * D ** -0.5),
        "ln1_g": 1.0 + nrm(ks[23], (L, D), 0.02),
        "ln1_b": nrm(ks[24], (L, D), 0.02),
        "w_up": nrm(ks[25], (L, D, 2 * D_FF), D ** -0.5),
        "conv_w": nrm(ks[26], (L, CONV_W, 1, D_FF), CONV_W ** -0.5),
        "conv_b": nrm(ks[27], (L, D_FF), 0.02),
        "w_down": nrm(ks[28], (L, D_FF, D), DEEPNORM_BETA * D_FF ** -0.5),
        "ln2_g": 1.0 + nrm(ks[29], (L, D), 0.02),
        "ln2_b": nrm(ks[30], (L, D), 0.02),
    }


def reference(x, c, w_mod, b_mod, w_in, ssm_lambda_re, ssm_lambda_im, ssm_log_dt, ssm_b_re, ssm_b_im,
              ssm_c_re, ssm_c_im, ssm_d, ssm_w_glu, ssm_b_glu, att_lambda_q1, att_lambda_k1,
              att_lambda_q2, att_lambda_k2, att_subln_g, w_branch_ssm, w_branch_att, w_out,
              ln1_g, ln1_b, w_up, conv_w, conv_b, w_down, ln2_g, ln2_b):
    bsz, seq, _ = x.shape
    cond = jax.nn.silu(c)
    splits = [SSM_WIDTH, SSM_WIDTH + ATT_QK_WIDTH, SSM_WIDTH + 2 * ATT_QK_WIDTH,
              SSM_WIDTH + 2 * ATT_QK_WIDTH + ATT_V_WIDTH,
              SSM_WIDTH + 2 * ATT_QK_WIDTH + ATT_V_WIDTH + D_MODEL]
    for l in range(DEPTH):
        lambda_init = 0.8 - 0.6 * math.exp(-0.3 * l)
        mod = (cond @ w_mod[l] + b_mod[l])[:, None, :]
        sh1, sc1, g1, sh2, sc2, g2 = jnp.split(mod, 6, axis=-1)

        h = _layernorm(x) * (1.0 + sc1) + sh1
        proj = h @ w_in[l]
        u, q, k, v, gs, ga = jnp.split(proj, splits, axis=-1)
        y_ssm = _s5_branch(u, ssm_lambda_re[l], ssm_lambda_im[l], ssm_log_dt[l], ssm_b_re[l],
                           ssm_b_im[l], ssm_c_re[l], ssm_c_im[l], ssm_d[l], ssm_w_glu[l], ssm_b_glu[l])
        lq1 = att_lambda_q1[l].astype(jnp.float32)
        lk1 = att_lambda_k1[l].astype(jnp.float32)
        lq2 = att_lambda_q2[l].astype(jnp.float32)
        lk2 = att_lambda_k2[l].astype(jnp.float32)
        lam = jnp.exp(jnp.sum(lq1 * lk1)) - jnp.exp(jnp.sum(lq2 * lk2)) + lambda_init
        y_att = _diff_attention(q.reshape(bsz, seq, ATT_HEADS, 2, ATT_HEAD_DIM),
                                k.reshape(bsz, seq, ATT_HEADS, 2, ATT_HEAD_DIM),
                                v.reshape(bsz, seq, ATT_HEADS, ATT_V_DIM),
                                lam, att_subln_g[l], lambda_init)
        merged = (jax.nn.sigmoid(gs) * (y_ssm @ w_branch_ssm[l])
                  + jax.nn.sigmoid(ga) * (y_att @ w_branch_att[l]))
        out = merged @ w_out[l]
        x = _layernorm(DEEPNORM_ALPHA * x + g1 * out) * ln1_g[l] + ln1_b[l]

        h2 = _layernorm(x) * (1.0 + sc2) + sh2
        a, val = jnp.split(h2 @ w_up[l], 2, axis=-1)
        a = lax.conv_general_dilated(a, conv_w[l], window_strides=(1,), padding=[(CONV_W - 1, 0)],
                                     dimension_numbers=('NWC', 'WIO', 'NWC'),
                                     feature_group_count=D_FF) + conv_b[l]
        f = (jax.nn.silu(a) * val) @ w_down[l]
        x = _layernorm(DEEPNORM_ALPHA * x + g2 * f) * ln2_g[l] + ln2_b[l]
    return x
```

```python
import functools
import math

import jax
import jax.numpy as jnp
from jax import lax
from jax.experimental import pallas as pl
from jax.experimental.pallas import tpu as pltpu

F32 = jnp.float32
BF16 = jnp.bfloat16

CHUNK = 64
ATT_HEADS = 8
ATT_HEAD_DIM = 64
ATT_V_DIM = 2 * ATT_HEAD_DIM
SSM_GROUP = 16
SSM_STATE = 64
CONV_W = 3
DEPTH = 1
DEEPNORM_ALPHA = (2.0 * DEPTH) ** 0.25
LN_EPS = 1e-5
RMS_EPS = 1e-5

def _lambda_init(layer):
    return 0.8 - 0.6 * math.exp(-0.3 * layer)


LANES = 128
VMEM_LIMIT_BYTES = 56 * 1024 * 1024

GROUP_TILE = 256
SCAN_LANES = 256
NEG_BIG = -1e30


def _layernorm(x):
    mu = jnp.mean(x, axis=-1, keepdims=True)
    xc = x - mu
    var = jnp.mean(xc * xc, axis=-1, keepdims=True)
    return xc * lax.rsqrt(var + LN_EPS)


def _const_spec(shape):
    nd = len(shape)
    return pl.BlockSpec(shape, lambda *_: (0,) * nd, pipeline_mode=pl.Buffered(1))


def _params(*sem):
    return pltpu.CompilerParams(dimension_semantics=sem, vmem_limit_bytes=VMEM_LIMIT_BYTES)


def _mod_kernel(c_ref, w_ref, b_ref, o_ref):
    cond = jax.nn.silu(c_ref[...]).astype(BF16)
    o_ref[...] = jnp.dot(cond, w_ref[...], preferred_element_type=F32) + b_ref[...]


def _mod(c, w_mod, b_mod):
    bsz, d = c.shape
    n = w_mod.shape[1]
    tn = d
    return pl.pallas_call(
        _mod_kernel,
        out_shape=jax.ShapeDtypeStruct((bsz, n), F32),
        grid=(n // tn,),
        in_specs=[pl.BlockSpec((bsz, d), lambda j: (0, 0)),
                  pl.BlockSpec((d, tn), lambda j: (0, j)),
                  pl.BlockSpec((1, tn), lambda j: (0, j))],
        out_specs=pl.BlockSpec((bsz, tn), lambda j: (0, j)),
        compiler_params=_params("arbitrary"),
        name="mod",
    )(c, w_mod, b_mod)


def _inproj_kernel(x_ref, mod_ref, w_ref, u_ref, q_ref, k_ref, v_ref, gs_ref, ga_ref, *, d, q_scale):
    h = _layernorm(x_ref[...])
    h = (h * (1.0 + mod_ref[:, d:2 * d]) + mod_ref[:, 0:d]).astype(BF16)

    def proj(n):
        return jnp.dot(h, w_ref[:, n * d:(n + 1) * d], preferred_element_type=F32)

    u_ref[...] = proj(0).astype(BF16)
    q_ref[...] = (proj(1) * q_scale).astype(BF16)
    k_ref[...] = proj(2).astype(BF16)
    v_ref[...] = proj(3).astype(BF16)
    gs_ref[...] = jax.nn.sigmoid(proj(4)).astype(BF16)
    ga_ref[...] = jax.nn.sigmoid(proj(5)).astype(BF16)


def _inproj(x, mod3, w_in, tm):
    bsz, seq, d = x.shape
    row = pl.BlockSpec((None, tm, d), lambda b, t: (b, t, 0))
    out_row = jax.ShapeDtypeStruct((bsz, seq, d), BF16)
    return pl.pallas_call(
        functools.partial(_inproj_kernel, d=d, q_scale=ATT_HEAD_DIM ** -0.5),
        out_shape=(jax.ShapeDtypeStruct((seq, bsz * d), BF16),) + (out_row,) * 5,
        grid=(bsz, seq // tm),
        in_specs=[row,
                  pl.BlockSpec((None, 1, mod3.shape[2]), lambda b, t: (b, 0, 0)),
                  _const_spec(w_in.shape)],
        out_specs=(pl.BlockSpec((tm, d), lambda b, t: (t, b)),) + (row,) * 5,
        compiler_params=_params("parallel", "parallel"),
        name="inproj",
    )(x, mod3, w_in)


def _ssm_kernel(u_ref, bblk_ref, cblk_ref, are_ref, aim_ref, d_ref, wglu_ref, bglu_ref, wbr_ref,
                out_ref, state_scr, bu_scr, sbf_scr, y_scr, *, ts, bsz, d):
    rows = ts * bsz
    n_tiles = d // GROUP_TILE
    st = GROUP_TILE * SSM_STATE // SSM_GROUP

    @pl.when(pl.program_id(0) == 0)
    def _():
        state_scr[...] = jnp.zeros_like(state_scr)

    u = u_ref[...].reshape(rows, d)
    for j in range(n_tiles):
        bu_scr[...] = jnp.dot(u[:, j * GROUP_TILE:(j + 1) * GROUP_TILE], bblk_ref[j],
                              preferred_element_type=F32)
        for c in range(st // SCAN_LANES):
            re = slice(c * SCAN_LANES, (c + 1) * SCAN_LANES)
            im = slice(st + c * SCAN_LANES, st + (c + 1) * SCAN_LANES)
            ar = jnp.broadcast_to(are_ref[j, :, re], (bsz, SCAN_LANES))
            ai = jnp.broadcast_to(aim_ref[j, :, re], (bsz, SCAN_LANES))

            def step(t, carry, re=re, im=im, ar=ar, ai=ai):
                sr, si = carry
                r0 = pl.multiple_of(t * bsz, bsz)
                nsr = ar * sr - ai * si + bu_scr[pl.ds(r0, bsz), re]
                nsi = ar * si + ai * sr + bu_scr[pl.ds(r0, bsz), im]
                sbf_scr[pl.ds(r0, bsz), re] = nsr.astype(BF16)
                sbf_scr[pl.ds(r0, bsz), im] = nsi.astype(BF16)
                return nsr, nsi

            sr, si = lax.fori_loop(0, ts, step, (state_scr[j, :, re], state_scr[j, :, im]), unroll=4)
            state_scr[j, :, re] = sr
            state_scr[j, :, im] = si
        y_scr[:, j * GROUP_TILE:(j + 1) * GROUP_TILE] = jnp.dot(
            sbf_scr[...], cblk_ref[j], preferred_element_type=F32)

    y = y_scr[...] + d_ref[...] * u.astype(F32)
    z = jax.nn.gelu(y)
    gate = jax.nn.sigmoid(jnp.dot(z.astype(BF16), wglu_ref[...], preferred_element_type=F32) + bglu_ref[...])
    ys = (z * gate).astype(BF16)
    out = jnp.dot(ys, wbr_ref[...], preferred_element_type=F32)
    out_ref[...] = out.astype(BF16).reshape(ts, bsz, d)


def _ssm(u_t, bblk, cblk, a_re, a_im, d_skip, w_glu, b_glu, w_br, ts):
    seq, bsz, d = u_t.shape
    rows = ts * bsz
    st2 = bblk.shape[2]
    blk = pl.BlockSpec((ts, bsz, d), lambda i: (i, 0, 0))
    return pl.pallas_call(
        functools.partial(_ssm_kernel, ts=ts, bsz=bsz, d=d),
        out_shape=jax.ShapeDtypeStruct((seq, bsz, d), BF16),
        grid=(seq // ts,),
        in_specs=[blk, _const_spec(bblk.shape), _const_spec(cblk.shape), _const_spec(a_re.shape),
                  _const_spec(a_im.shape), _const_spec(d_skip.shape), _const_spec(w_glu.shape),
                  _const_spec(b_glu.shape), _const_spec(w_br.shape)],
        out_specs=blk,
        scratch_shapes=[pltpu.VMEM((bblk.shape[0], bsz, st2), F32),
                        pltpu.VMEM((rows, st2), F32),
                        pltpu.VMEM((rows, st2), BF16),
                        pltpu.VMEM((rows, d), F32)],
        compiler_params=_params("arbitrary"),
        name="ssm",
    )(u_t, bblk, cblk, a_re, a_im, d_skip, w_glu, b_glu, w_br)


def _ssm_tables(lam_re, lam_im, log_dt, b_re, b_im, c_re, c_im):
    g, p = lam_re.shape
    gpt = GROUP_TILE // SSM_GROUP
    nt = g // gpt
    dt = jnp.exp(log_dt)[:, None]
    mag = jnp.exp(lam_re * dt)
    ab_re = mag * jnp.cos(lam_im * dt)
    ab_im = mag * jnp.sin(lam_im * dt)
    den = lam_re * lam_re + lam_im * lam_im
    nr = ab_re - 1.0
    ni = ab_im
    coef_re = (nr * lam_re + ni * lam_im) / den
    coef_im = (ni * lam_re - nr * lam_im) / den
    bb_re = coef_re[..., None] * b_re - coef_im[..., None] * b_im
    bb_im = coef_re[..., None] * b_im + coef_im[..., None] * b_re
    eye = jnp.eye(gpt, dtype=F32)

    def in_blk(bb):
        t = bb.reshape(nt, gpt, p, SSM_GROUP)
        return jnp.einsum('jgpc,gh->jgchp', t, eye).reshape(nt, gpt * SSM_GROUP, gpt * p)

    def out_blk(cc):
        t = cc.reshape(nt, gpt, SSM_GROUP, p)
        return jnp.einsum('jgcp,gh->jgphc', t, eye).reshape(nt, gpt * p, gpt * SSM_GROUP)

    bblk = jnp.concatenate([in_blk(bb_re), in_blk(bb_im)], axis=2).astype(BF16)
    cblk = jnp.concatenate([out_blk(c_re), -out_blk(c_im)], axis=1).astype(BF16)
    a_re = ab_re.reshape(nt, 1, gpt * p)
    a_im = ab_im.reshape(nt, 1, gpt * p)
    return bblk, cblk, a_re, a_im


def _attn_kernel(lam_ref, q_ref, k_ref, v_ref, g_ref, o_ref, m_scr, l_scr, acc_scr, *, tq, lambda_init):
    qi = pl.program_id(2)
    q = q_ref[...]
    lane = lax.broadcasted_iota(jnp.int32, q.shape, 1)
    zero = jnp.zeros_like(q)
    qs = jnp.concatenate([jnp.where(lane < ATT_HEAD_DIM, q, zero),
                          jnp.where(lane >= ATT_HEAD_DIM, q, zero)], axis=0)

    m_scr[...] = jnp.full_like(m_scr, NEG_BIG)
    l_scr[...] = jnp.zeros_like(l_scr)
    acc_scr[...] = jnp.zeros_like(acc_scr)

    def block(j, mask):
        r0 = pl.multiple_of(j * tq, tq)
        s = lax.dot_general(qs, k_ref[pl.ds(r0, tq), :], (((1,), (1,)), ((), ())),
                            preferred_element_type=F32)
        if mask is not None:
            s = jnp.where(mask, s, NEG_BIG)
        m_prev = m_scr[...]
        m_new = jnp.maximum(m_prev, jnp.max(s, axis=1, keepdims=True))
        p = jnp.exp(s - m_new)
        alpha = jnp.exp(m_prev - m_new)
        l_scr[...] = alpha * l_scr[...] + jnp.sum(p, axis=1, keepdims=True)
        acc_scr[...] = alpha * acc_scr[...] + jnp.dot(p.astype(BF16), v_ref[pl.ds(r0, tq), :],
                                                      preferred_element_type=F32)
        m_scr[...] = m_new

    def body(j, carry):
        block(j, None)
        return carry

    lax.fori_loop(0, qi, body, 0)
    row_chunk = (lax.broadcasted_iota(jnp.int32, (2 * tq, tq), 0) % tq) // CHUNK
    col_chunk = lax.broadcasted_iota(jnp.int32, (2 * tq, tq), 1) // CHUNK
    block(qi, col_chunk <= row_chunk)

    o = acc_scr[...] / l_scr[...]
    o = o[:tq] - lam_ref[0] * o[tq:]
    o = o * lax.rsqrt(jnp.mean(o * o, axis=-1, keepdims=True) + RMS_EPS)
    o_ref[...] = ((o * g_ref[...]) * (1.0 - lambda_init)).astype(o_ref.dtype)


def _attention(lam, q, k, v, subln_g, tq, lambda_init):
    bsz, seq, w = q.shape
    hd = 2 * ATT_HEAD_DIM
    return pl.pallas_call(
        functools.partial(_attn_kernel, tq=tq, lambda_init=lambda_init),
        out_shape=jax.ShapeDtypeStruct((bsz, seq, w), BF16),
        grid=(bsz, ATT_HEADS, seq // tq),
        in_specs=[pl.BlockSpec(memory_space=pltpu.SMEM),
                  pl.BlockSpec((None, tq, hd), lambda b, h, i: (b, i, h)),
                  pl.BlockSpec((None, seq, hd), lambda b, h, i: (b, 0, h)),
                  pl.BlockSpec((None, seq, ATT_V_DIM), lambda b, h, i: (b, 0, h)),
                  pl.BlockSpec((1, ATT_V_DIM), lambda b, h, i: (0, 0))],
        out_specs=pl.BlockSpec((None, tq, ATT_V_DIM), lambda b, h, i: (b, i, h)),
        scratch_shapes=[pltpu.VMEM((2 * tq, 1), F32), pltpu.VMEM((2 * tq, 1), F32),
                        pltpu.VMEM((2 * tq, ATT_V_DIM), F32)],
        compiler_params=_params("parallel", "parallel", "arbitrary"),
        name="attn",
    )(lam, q, k, v, subln_g)


def _merge_kernel(x_ref, ys_ref, ya_ref, gs_ref, ga_ref, mod_ref, wba_ref, wout_ref, g_ref, b_ref,
                  o_ref, *, d):
    att = jnp.dot(ya_ref[...], wba_ref[...], preferred_element_type=F32)
    merged = gs_ref[...].astype(F32) * ys_ref[...].astype(F32) + ga_ref[...].astype(F32) * att
    out = jnp.dot(merged.astype(BF16), wout_ref[...], preferred_element_type=F32)
    g1 = mod_ref[:, 2 * d:3 * d]
    o_ref[...] = _layernorm(DEEPNORM_ALPHA * x_ref[...] + g1 * out) * g_ref[...] + b_ref[...]


def _merge(x, ys2d, ya, gs, ga, mod3, w_ba, w_out, ln_g, ln_b, tm):
    bsz, seq, d = x.shape
    row = pl.BlockSpec((None, tm, d), lambda b, t: (b, t, 0))
    return pl.pallas_call(
        functools.partial(_merge_kernel, d=d),
        out_shape=jax.ShapeDtypeStruct((bsz, seq, d), F32),
        grid=(bsz, seq // tm),
        in_specs=[row, pl.BlockSpec((tm, d), lambda b, t: (t, b)), row, row, row,
                  pl.BlockSpec((None, 1, mod3.shape[2]), lambda b, t: (b, 0, 0)),
                  _const_spec(w_ba.shape), _const_spec(w_out.shape),
                  _const_spec(ln_g.shape), _const_spec(ln_b.shape)],
        out_specs=row,
        compiler_params=_params("parallel", "parallel"),
        name="merge",
    )(x, ys2d, ya, gs, ga, mod3, w_ba, w_out, ln_g, ln_b)


def _ffn_kernel(x_ref, mod_ref, wup_ref, cw_ref, cb_ref, wdn_ref, g_ref, b_ref, o_ref, carry_scr,
                *, d, dff, n_chunks):
    tm = x_ref.shape[0]
    tc = dff // n_chunks

    @pl.when(pl.program_id(1) == 0)
    def _():
        carry_scr[...] = jnp.zeros_like(carry_scr)

    x = x_ref[...]
    h = (_layernorm(x) * (1.0 + mod_ref[:, 4 * d:5 * d]) + mod_ref[:, 3 * d:4 * d]).astype(BF16)
    row = lax.broadcasted_iota(jnp.int32, (tm, tc), 0)
    f = jnp.zeros((tm, d), F32)
    for c in range(n_chunks):
        cols = slice(c * tc, (c + 1) * tc)
        a = jnp.dot(h, wup_ref[:, cols], preferred_element_type=F32)
        val = jnp.dot(h, wup_ref[:, dff + c * tc:dff + (c + 1) * tc], preferred_element_type=F32)
        prev = carry_scr[:, cols]
        p1 = jnp.broadcast_to(prev[7:8], (tm, tc))
        p2 = jnp.broadcast_to(prev[6:7], (tm, tc))
        a1 = jnp.where(row == 0, p1, pltpu.roll(a, 1, 0))
        a2 = jnp.where(row == 0, p2, jnp.where(row == 1, p1, pltpu.roll(a, 2, 0)))
        carry_scr[:, cols] = a[tm - 8:tm]
        ac = cw_ref[0:1, cols] * a2 + cw_ref[1:2, cols] * a1 + cw_ref[2:3, cols] * a + cb_ref[:, cols]
        gl = (jax.nn.silu(ac) * val).astype(BF16)
        f = f + jnp.dot(gl, wdn_ref[cols, :], preferred_element_type=F32)
    g2 = mod_ref[:, 5 * d:6 * d]
    o_ref[...] = _layernorm(DEEPNORM_ALPHA * x + g2 * f) * g_ref[...] + b_ref[...]


def _ffn(x, mod3, w_up, conv_w, conv_b, w_down, ln_g, ln_b, tm):
    bsz, seq, d = x.shape
    dff = w_down.shape[0]
    row = pl.BlockSpec((None, tm, d), lambda b, t: (b, t, 0))
    return pl.pallas_call(
        functools.partial(_ffn_kernel, d=d, dff=dff, n_chunks=2),
        out_shape=jax.ShapeDtypeStruct((bsz, seq, d), F32),
        grid=(bsz, seq // tm),
        in_specs=[row, pl.BlockSpec((None, 1, mod3.shape[2]), lambda b, t: (b, 0, 0)),
                  _const_spec(w_up.shape), _const_spec(conv_w.shape), _const_spec(conv_b.shape),
                  _const_spec(w_down.shape), _const_spec(ln_g.shape), _const_spec(ln_b.shape)],
        out_specs=row,
        scratch_shapes=[pltpu.VMEM((8, dff), F32)],
        compiler_params=_params("parallel", "arbitrary"),
        name="ffn",
    )(x, mod3, w_up, conv_w, conv_b, w_down, ln_g, ln_b)


def kernel(x, c, w_mod, b_mod, w_in, ssm_lambda_re, ssm_lambda_im, ssm_log_dt, ssm_b_re, ssm_b_im, ssm_c_re, ssm_c_im, ssm_d, ssm_w_glu, ssm_b_glu, att_lambda_q1, att_lambda_k1, att_lambda_q2, att_lambda_k2, att_subln_g, w_branch_ssm, w_branch_att, w_out, ln1_g, ln1_b, w_up, conv_w, conv_b, w_down, ln2_g, ln2_b):
    bsz, seq, d = x.shape
    depth = w_mod.shape[0]
    assert depth == DEPTH
    tm = min(256, seq)
    ts = min(32, seq)
    tq = min(256, seq)
    for l in range(depth):
        mod3 = _mod(c, w_mod[l].astype(BF16), b_mod[l][None, :])[:, None, :]
        u2d, q, k, v, gs, ga = _inproj(x, mod3, w_in[l].astype(BF16), tm)

        bblk, cblk, a_re, a_im = _ssm_tables(ssm_lambda_re[l], ssm_lambda_im[l], ssm_log_dt[l],
                                             ssm_b_re[l], ssm_b_im[l], ssm_c_re[l], ssm_c_im[l])
        ys_t = _ssm(u2d.reshape(seq, bsz, d), bblk, cblk, a_re, a_im, ssm_d[l][None, :],
                    ssm_w_glu[l].astype(BF16), ssm_b_glu[l][None, :], w_branch_ssm[l].astype(BF16), ts)

        lam = (jnp.exp(jnp.sum(att_lambda_q1[l] * att_lambda_k1[l]))
               - jnp.exp(jnp.sum(att_lambda_q2[l] * att_lambda_k2[l])) + _lambda_init(l))
        ya = _attention(lam.reshape(1).astype(F32), q, k, v, att_subln_g[l][None, :], tq,
                        _lambda_init(l))

        x = _merge(x, ys_t.reshape(seq, bsz * d), ya, gs, ga, mod3, w_branch_att[l].astype(BF16),
                   w_out[l].astype(BF16), ln1_g[l][None, :], ln1_b[l][None, :], tm)
        x = _ffn(x, mod3, w_up[l].astype(BF16), conv_w[l].reshape(CONV_W, -1), conv_b[l][None, :],
                 w_down[l].astype(BF16), ln2_g[l][None, :], ln2_b[l][None, :], tm)
    return x
```

```python
import functools
import math

import jax
import jax.numpy as jnp
from jax import lax
from jax.experimental import pallas as pl
from jax.experimental.pallas import tpu as pltpu

F32 = jnp.float32
BF16 = jnp.bfloat16

CHUNK = 64
ATT_HEADS = 8
ATT_HEAD_DIM = 64
ATT_V_DIM = 2 * ATT_HEAD_DIM
SSM_GROUP = 16
SSM_STATE = 64
CONV_W = 3
DEPTH = 1
DEEPNORM_ALPHA = (2.0 * DEPTH) ** 0.25
LN_EPS = 1e-5
RMS_EPS = 1e-5

def _lambda_init(layer):
    return 0.8 - 0.6 * math.exp(-0.3 * layer)


LANES = 128
MXU_WIDTH = 256
VMEM_LIMIT_BYTES = 56 * 1024 * 1024

GROUP_TILE = 256
SCAN_LANES = 256
NEG_BIG = -1e30


def _layernorm(x):
    mu = jnp.mean(x, axis=-1, keepdims=True)
    xc = x - mu
    var = jnp.mean(xc * xc, axis=-1, keepdims=True)
    return xc * lax.rsqrt(var + LN_EPS)


def _const_spec(shape):
    nd = len(shape)
    return pl.BlockSpec(shape, lambda *_: (0,) * nd, pipeline_mode=pl.Buffered(1))


def _params(*sem):
    return pltpu.CompilerParams(dimension_semantics=sem, vmem_limit_bytes=VMEM_LIMIT_BYTES)


def _mod_kernel(c_ref, w_ref, b_ref, o_ref):
    cond = jax.nn.silu(c_ref[...]).astype(BF16)
    o_ref[...] = jnp.dot(cond, w_ref[...], preferred_element_type=F32) + b_ref[...]


def _mod(c, w_mod, b_mod):
    bsz, d = c.shape
    n = w_mod.shape[1]
    tn = d
    return pl.pallas_call(
        _mod_kernel,
        out_shape=jax.ShapeDtypeStruct((bsz, n), F32),
        grid=(n // tn,),
        in_specs=[pl.BlockSpec((bsz, d), lambda j: (0, 0)),
                  pl.BlockSpec((d, tn), lambda j: (0, j)),
                  pl.BlockSpec((1, tn), lambda j: (0, j))],
        out_specs=pl.BlockSpec((bsz, tn), lambda j: (0, j)),
        compiler_params=_params("arbitrary"),
        name="mod",
    )(c, w_mod, b_mod)


def _inproj_kernel(x_ref, mod_ref, w_ref, u_ref, q_ref, k_ref, v_ref, gs_ref, ga_ref, *, d, q_scale):
    h = _layernorm(x_ref[...])
    h = (h * (1.0 + mod_ref[:, d:2 * d]) + mod_ref[:, 0:d]).astype(BF16)

    def proj(n):
        return jnp.dot(h, w_ref[:, n * d:(n + 1) * d], preferred_element_type=F32)

    u_ref[...] = proj(0).astype(BF16)
    q_ref[...] = (proj(1) * q_scale).astype(BF16)
    k_ref[...] = proj(2).astype(BF16)
    v_ref[...] = proj(3).astype(BF16)
    gs_ref[...] = jax.nn.sigmoid(proj(4)).astype(BF16)
    ga_ref[...] = jax.nn.sigmoid(proj(5)).astype(BF16)


def _inproj(x, mod3, w_in, tm):
    bsz, seq, d = x.shape
    row = pl.BlockSpec((None, tm, d), lambda b, t: (b, t, 0))
    out_row = jax.ShapeDtypeStruct((bsz, seq, d), BF16)
    return pl.pallas_call(
        functools.partial(_inproj_kernel, d=d, q_scale=ATT_HEAD_DIM ** -0.5),
        out_shape=(jax.ShapeDtypeStruct((seq, bsz * d), BF16),) + (out_row,) * 5,
        grid=(bsz, seq // tm),
        in_specs=[row,
                  pl.BlockSpec((None, 1, mod3.shape[2]), lambda b, t: (b, 0, 0)),
                  _const_spec(w_in.shape)],
        out_specs=(pl.BlockSpec((tm, d), lambda b, t: (t, b)),) + (row,) * 5,
        compiler_params=_params("parallel", "parallel"),
        name="inproj",
    )(x, mod3, w_in)


def _ssm_kernel(u_ref, bblk_ref, cblk_ref, are_ref, aim_ref, d_ref, wglu_ref, bglu_ref, wbr_ref,
                out_ref, state_scr, bu_scr, sbf_scr, y_scr, *, ts, bsz, d):
    rows = ts * bsz
    n_tiles = d // GROUP_TILE
    st = GROUP_TILE * SSM_STATE // SSM_GROUP

    @pl.when(pl.program_id(0) == 0)
    def _():
        state_scr[...] = jnp.zeros_like(state_scr)

    u = u_ref[...].reshape(rows, d)
    for j in range(n_tiles):
        bu_scr[...] = jnp.dot(u[:, j * GROUP_TILE:(j + 1) * GROUP_TILE], bblk_ref[j],
                              preferred_element_type=F32)
        for c in range(st // SCAN_LANES):
            re = slice(c * SCAN_LANES, (c + 1) * SCAN_LANES)
            im = slice(st + c * SCAN_LANES, st + (c + 1) * SCAN_LANES)
            ar = jnp.broadcast_to(are_ref[j, :, re], (bsz, SCAN_LANES))
            ai = jnp.broadcast_to(aim_ref[j, :, re], (bsz, SCAN_LANES))

            def step(t, carry, re=re, im=im, ar=ar, ai=ai):
                sr, si = carry
                r0 = pl.multiple_of(t * bsz, bsz)
                nsr = ar * sr - ai * si + bu_scr[pl.ds(r0, bsz), re]
                nsi = ar * si + ai * sr + bu_scr[pl.ds(r0, bsz), im]
                sbf_scr[pl.ds(r0, bsz), re] = nsr.astype(BF16)
                sbf_scr[pl.ds(r0, bsz), im] = nsi.astype(BF16)
                return nsr, nsi

            sr, si = lax.fori_loop(0, ts, step, (state_scr[j, :, re], state_scr[j, :, im]), unroll=4)
            state_scr[j, :, re] = sr
            state_scr[j, :, im] = si
        y_scr[:, j * GROUP_TILE:(j + 1) * GROUP_TILE] = jnp.dot(
            sbf_scr[...], cblk_ref[j], preferred_element_type=F32)

    y = y_scr[...] + d_ref[...] * u.astype(F32)
    z = jax.nn.gelu(y)
    gate = jax.nn.sigmoid(jnp.dot(z.astype(BF16), wglu_ref[...], preferred_element_type=F32) + bglu_ref[...])
    ys = (z * gate).astype(BF16)
    out = jnp.dot(ys, wbr_ref[...], preferred_element_type=F32)
    out_ref[...] = out.astype(BF16).reshape(ts, bsz, d)


def _ssm(u_t, bblk, cblk, a_re, a_im, d_skip, w_glu, b_glu, w_br, ts):
    seq, bsz, d = u_t.shape
    rows = ts * bsz
    st2 = bblk.shape[2]
    blk = pl.BlockSpec((ts, bsz, d), lambda i: (i, 0, 0))
    return pl.pallas_call(
        functools.partial(_ssm_kernel, ts=ts, bsz=bsz, d=d),
        out_shape=jax.ShapeDtypeStruct((seq, bsz, d), BF16),
        grid=(seq // ts,),
        in_specs=[blk, _const_spec(bblk.shape), _const_spec(cblk.shape), _const_spec(a_re.shape),
                  _const_spec(a_im.shape), _const_spec(d_skip.shape), _const_spec(w_glu.shape),
                  _const_spec(b_glu.shape), _const_spec(w_br.shape)],
        out_specs=blk,
        scratch_shapes=[pltpu.VMEM((bblk.shape[0], bsz, st2), F32),
                        pltpu.VMEM((rows, st2), F32),
                        pltpu.VMEM((rows, st2), BF16),
                        pltpu.VMEM((rows, d), F32)],
        compiler_params=_params("arbitrary"),
        name="ssm",
    )(u_t, bblk, cblk, a_re, a_im, d_skip, w_glu, b_glu, w_br)


def _ssm_tables(lam_re, lam_im, log_dt, b_re, b_im, c_re, c_im):
    g, p = lam_re.shape
    gpt = GROUP_TILE // SSM_GROUP
    nt = g // gpt
    dt = jnp.exp(log_dt)[:, None]
    mag = jnp.exp(lam_re * dt)
    ab_re = mag * jnp.cos(lam_im * dt)
    ab_im = mag * jnp.sin(lam_im * dt)
    den = lam_re * lam_re + lam_im * lam_im
    nr = ab_re - 1.0
    ni = ab_im
    coef_re = (nr * lam_re + ni * lam_im) / den
    coef_im = (ni * lam_re - nr * lam_im) / den
    bb_re = coef_re[..., None] * b_re - coef_im[..., None] * b_im
    bb_im = coef_re[..., None] * b_im + coef_im[..., None] * b_re
    eye = jnp.eye(gpt, dtype=F32)

    def in_blk(bb):
        t = bb.reshape(nt, gpt, p, SSM_GROUP)
        return jnp.einsum('jgpc,gh->jgchp', t, eye).reshape(nt, gpt * SSM_GROUP, gpt * p)

    def out_blk(cc):
        t = cc.reshape(nt, gpt, SSM_GROUP, p)
        return jnp.einsum('jgcp,gh->jgphc', t, eye).reshape(nt, gpt * p, gpt * SSM_GROUP)

    bblk = jnp.concatenate([in_blk(bb_re), in_blk(bb_im)], axis=2).astype(BF16)
    cblk = jnp.concatenate([out_blk(c_re), -out_blk(c_im)], axis=1).astype(BF16)
    a_re = ab_re.reshape(nt, 1, gpt * p)
    a_im = ab_im.reshape(nt, 1, gpt * p)
    return bblk, cblk, a_re, a_im


def _attn_kernel(lam_ref, q_ref, k_ref, v_ref, g_ref, o_ref, vext_scr, s0_scr, s1_scr, m_scr, acc_scr,
                 *, tq, lambda_init):
    qi = pl.program_id(2)
    vd = ATT_V_DIM

    @pl.when(qi == 0)
    def _():
        vext_scr[:, :vd] = v_ref[...]
        vext_scr[:, vd:] = jnp.ones((vext_scr.shape[0], vext_scr.shape[1] - vd), vext_scr.dtype)

    q = q_ref[...]
    lane = lax.broadcasted_iota(jnp.int32, q.shape, 1)
    zero = jnp.zeros_like(q)
    qs = jnp.concatenate([jnp.where(lane < ATT_HEAD_DIM, q, zero),
                          jnp.where(lane >= ATT_HEAD_DIM, q, zero)], axis=0)

    m_scr[...] = jnp.full_like(m_scr, NEG_BIG)
    acc_scr[...] = jnp.zeros_like(acc_scr)

    def scores(j):
        r0 = pl.multiple_of(j * tq, tq)
        return lax.dot_general(qs, k_ref[pl.ds(r0, tq), :], (((1,), (1,)), ((), ())),
                               preferred_element_type=F32)

    def update(s, j):
        r0 = pl.multiple_of(j * tq, tq)
        m_prev = m_scr[...]
        m_new = jnp.maximum(m_prev, jnp.max(s, axis=1, keepdims=True))
        p = jnp.exp(s - pltpu.repeat(m_new, tq // LANES, 1))
        alpha = jnp.exp(m_prev - m_new)
        pv = jnp.dot(p.astype(BF16), vext_scr[pl.ds(r0, tq), :], preferred_element_type=F32)
        acc_scr[...] = pltpu.repeat(alpha, acc_scr.shape[1] // LANES, 1) * acc_scr[...] + pv
        m_scr[...] = m_new

    def masked(s):
        row = lax.broadcasted_iota(jnp.int32, (2 * tq, 1), 0)
        limit = ((row % tq) // CHUNK + 1) * CHUNK
        col = lax.broadcasted_iota(jnp.int32, s.shape, 1)
        return jnp.where(col < limit, s, NEG_BIG)

    s0_scr[...] = scores(0)

    def body(j, carry):
        @pl.when(j % 2 == 0)
        def _():
            s1_scr[...] = scores(j + 1)
            update(s0_scr[...], j)

        @pl.when(j % 2 == 1)
        def _():
            s0_scr[...] = scores(j + 1)
            update(s1_scr[...], j)

        return carry

    lax.fori_loop(0, qi, body, 0)

    @pl.when(qi % 2 == 0)
    def _():
        update(masked(s0_scr[...]), qi)

    @pl.when(qi % 2 == 1)
    def _():
        update(masked(s1_scr[...]), qi)

    acc = acc_scr[...]
    o = acc[:, :vd] / acc[:, vd:2 * vd]
    o = o[:tq] - lam_ref[0] * o[tq:]
    o = o * lax.rsqrt(jnp.mean(o * o, axis=-1, keepdims=True) + RMS_EPS)
    o_ref[...] = ((o * g_ref[...]) * (1.0 - lambda_init)).astype(o_ref.dtype)


def _attention(lam, q, k, v, subln_g, tq, lambda_init):
    bsz, seq, w = q.shape
    hd = 2 * ATT_HEAD_DIM
    return pl.pallas_call(
        functools.partial(_attn_kernel, tq=tq, lambda_init=lambda_init),
        out_shape=jax.ShapeDtypeStruct((bsz, seq, w), BF16),
        grid=(bsz, ATT_HEADS, seq // tq),
        in_specs=[pl.BlockSpec(memory_space=pltpu.SMEM),
                  pl.BlockSpec((None, tq, hd), lambda b, h, i: (b, i, h)),
                  pl.BlockSpec((None, seq, hd), lambda b, h, i: (b, 0, h)),
                  pl.BlockSpec((None, seq, ATT_V_DIM), lambda b, h, i: (b, 0, h)),
                  pl.BlockSpec((1, ATT_V_DIM), lambda b, h, i: (0, 0))],
        out_specs=pl.BlockSpec((None, tq, ATT_V_DIM), lambda b, h, i: (b, i, h)),
        scratch_shapes=[pltpu.VMEM((seq, MXU_WIDTH), BF16),
                        pltpu.VMEM((2 * tq, tq), F32), pltpu.VMEM((2 * tq, tq), F32),
                        pltpu.VMEM((2 * tq, LANES), F32),
                        pltpu.VMEM((2 * tq, MXU_WIDTH), F32)],
        compiler_params=_params("parallel", "parallel", "arbitrary"),
        name="attn",
    )(lam, q, k, v, subln_g)


def _merge_kernel(x_ref, ys_ref, ya_ref, gs_ref, ga_ref, mod_ref, wba_ref, wout_ref, g_ref, b_ref,
                  o_ref, *, d):
    att = jnp.dot(ya_ref[...], wba_ref[...], preferred_element_type=F32)
    merged = gs_ref[...].astype(F32) * ys_ref[...].astype(F32) + ga_ref[...].astype(F32) * att
    out = jnp.dot(merged.astype(BF16), wout_ref[...], preferred_element_type=F32)
    g1 = mod_ref[:, 2 * d:3 * d]
    o_ref[...] = _layernorm(DEEPNORM_ALPHA * x_ref[...] + g1 * out) * g_ref[...] + b_ref[...]


def _merge(x, ys2d, ya, gs, ga, mod3, w_ba, w_out, ln_g, ln_b, tm):
    bsz, seq, d = x.shape
    row = pl.BlockSpec((None, tm, d), lambda b, t: (b, t, 0))
    return pl.pallas_call(
        functools.partial(_merge_kernel, d=d),
        out_shape=jax.ShapeDtypeStruct((bsz, seq, d), F32),
        grid=(bsz, seq // tm),
        in_specs=[row, pl.BlockSpec((tm, d), lambda b, t: (t, b)), row, row, row,
                  pl.BlockSpec((None, 1, mod3.shape[2]), lambda b, t: (b, 0, 0)),
                  _const_spec(w_ba.shape), _const_spec(w_out.shape),
                  _const_spec(ln_g.shape), _const_spec(ln_b.shape)],
        out_specs=row,
        compiler_params=_params("parallel", "parallel"),
        name="merge",
    )(x, ys2d, ya, gs, ga, mod3, w_ba, w_out, ln_g, ln_b)


def _ffn_kernel(x_ref, mod_ref, wup_ref, cw_ref, cb_ref, wdn_ref, g_ref, b_ref, o_ref, carry_scr,
                *, d, dff, n_chunks):
    tm = x_ref.shape[0]
    tc = dff // n_chunks

    @pl.when(pl.program_id(1) == 0)
    def _():
        carry_scr[...] = jnp.zeros_like(carry_scr)

    x = x_ref[...]
    h = (_layernorm(x) * (1.0 + mod_ref[:, 4 * d:5 * d]) + mod_ref[:, 3 * d:4 * d]).astype(BF16)
    row = lax.broadcasted_iota(jnp.int32, (tm, tc), 0)
    f = jnp.zeros((tm, d), F32)
    for c in range(n_chunks):
        cols = slice(c * tc, (c + 1) * tc)
        a = jnp.dot(h, wup_ref[:, cols], preferred_element_type=F32)
        val = jnp.dot(h, wup_ref[:, dff + c * tc:dff + (c + 1) * tc], preferred_element_type=F32)
        prev = carry_scr[:, cols]
        p1 = jnp.broadcast_to(prev[7:8], (tm, tc))
        p2 = jnp.broadcast_to(prev[6:7], (tm, tc))
        a1 = jnp.where(row == 0, p1, pltpu.roll(a, 1, 0))
        a2 = jnp.where(row == 0, p2, jnp.where(row == 1, p1, pltpu.roll(a, 2, 0)))
        carry_scr[:, cols] = a[tm - 8:tm]
        ac = cw_ref[0:1, cols] * a2 + cw_ref[1:2, cols] * a1 + cw_ref[2:3, cols] * a + cb_ref[:, cols]
        gl = (jax.nn.silu(ac) * val).astype(BF16)
        f = f + jnp.dot(gl, wdn_ref[cols, :], preferred_element_type=F32)
    g2 = mod_ref[:, 5 * d:6 * d]
    o_ref[...] = _layernorm(DEEPNORM_ALPHA * x + g2 * f) * g_ref[...] + b_ref[...]


def _ffn(x, mod3, w_up, conv_w, conv_b, w_down, ln_g, ln_b, tm):
    bsz, seq, d = x.shape
    dff = w_down.shape[0]
    row = pl.BlockSpec((None, tm, d), lambda b, t: (b, t, 0))
    return pl.pallas_call(
        functools.partial(_ffn_kernel, d=d, dff=dff, n_chunks=2),
        out_shape=jax.ShapeDtypeStruct((bsz, seq, d), F32),
        grid=(bsz, seq // tm),
        in_specs=[row, pl.BlockSpec((None, 1, mod3.shape[2]), lambda b, t: (b, 0, 0)),
                  _const_spec(w_up.shape), _const_spec(conv_w.shape), _const_spec(conv_b.shape),
                  _const_spec(w_down.shape), _const_spec(ln_g.shape), _const_spec(ln_b.shape)],
        out_specs=row,
        scratch_shapes=[pltpu.VMEM((8, dff), F32)],
        compiler_params=_params("parallel", "arbitrary"),
        name="ffn",
    )(x, mod3, w_up, conv_w, conv_b, w_down, ln_g, ln_b)


def kernel(x, c, w_mod, b_mod, w_in, ssm_lambda_re, ssm_lambda_im, ssm_log_dt, ssm_b_re, ssm_b_im, ssm_c_re, ssm_c_im, ssm_d, ssm_w_glu, ssm_b_glu, att_lambda_q1, att_lambda_k1, att_lambda_q2, att_lambda_k2, att_subln_g, w_branch_ssm, w_branch_att, w_out, ln1_g, ln1_b, w_up, conv_w, conv_b, w_down, ln2_g, ln2_b):
    bsz, seq, d = x.shape
    depth = w_mod.shape[0]
    assert depth == DEPTH
    tm = min(256, seq)
    ts = min(32, seq)
    tq = min(512, seq)
    for l in range(depth):
        mod3 = _mod(c, w_mod[l].astype(BF16), b_mod[l][None, :])[:, None, :]
        u2d, q, k, v, gs, ga = _inproj(x, mod3, w_in[l].astype(BF16), tm)

        bblk, cblk, a_re, a_im = _ssm_tables(ssm_lambda_re[l], ssm_lambda_im[l], ssm_log_dt[l],
                                             ssm_b_re[l], ssm_b_im[l], ssm_c_re[l], ssm_c_im[l])
        ys_t = _ssm(u2d.reshape(seq, bsz, d), bblk, cblk, a_re, a_im, ssm_d[l][None, :],
                    ssm_w_glu[l].astype(BF16), ssm_b_glu[l][None, :], w_branch_ssm[l].astype(BF16), ts)

        lam = (jnp.exp(jnp.sum(att_lambda_q1[l] * att_lambda_k1[l]))
               - jnp.exp(jnp.sum(att_lambda_q2[l] * att_lambda_k2[l])) + _lambda_init(l))
        ya = _attention(lam.reshape(1).astype(F32), q, k, v, att_subln_g[l][None, :], tq,
                        _lambda_init(l))

        x = _merge(x, ys_t.reshape(seq, bsz * d), ya, gs, ga, mod3, w_branch_att[l].astype(BF16),
                   w_out[l].astype(BF16), ln1_g[l][None, :], ln1_b[l][None, :], tm)
        x = _ffn(x, mod3, w_up[l].astype(BF16), conv_w[l].reshape(CONV_W, -1), conv_b[l][None, :],
                 w_down[l].astype(BF16), ln2_g[l][None, :], ln2_b[l][None, :], tm)
    return x
```

```python
import functools
import math

import jax
import jax.numpy as jnp
from jax import lax
from jax.experimental import pallas as pl
from jax.experimental.pallas import tpu as pltpu

F32 = jnp.float32
BF16 = jnp.bfloat16

CHUNK = 64
ATT_HEADS = 8
ATT_HEAD_DIM = 64
ATT_V_DIM = 2 * ATT_HEAD_DIM
SSM_GROUP = 16
SSM_STATE = 64
CONV_W = 3
DEPTH = 1
DEEPNORM_ALPHA = (2.0 * DEPTH) ** 0.25
LN_EPS = 1e-5
RMS_EPS = 1e-5

def _lambda_init(layer):
    return 0.8 - 0.6 * math.exp(-0.3 * layer)


LANES = 128
MXU_WIDTH = 256
VMEM_LIMIT_BYTES = 56 * 1024 * 1024

GROUP_TILE = 256
SCAN_LANES = 256
NEG_BIG = -1e30


def _layernorm(x):
    mu = jnp.mean(x, axis=-1, keepdims=True)
    xc = x - mu
    var = jnp.mean(xc * xc, axis=-1, keepdims=True)
    return xc * lax.rsqrt(var + LN_EPS)


def _const_spec(shape):
    nd = len(shape)
    return pl.BlockSpec(shape, lambda *_: (0,) * nd, pipeline_mode=pl.Buffered(1))


def _params(*sem):
    return pltpu.CompilerParams(dimension_semantics=sem, vmem_limit_bytes=VMEM_LIMIT_BYTES)


def _mod_kernel(c_ref, w_ref, b_ref, o_ref):
    cond = jax.nn.silu(c_ref[...]).astype(BF16)
    o_ref[...] = jnp.dot(cond, w_ref[...], preferred_element_type=F32) + b_ref[...]


def _mod(c, w_mod, b_mod):
    bsz, d = c.shape
    n = w_mod.shape[1]
    tn = d
    return pl.pallas_call(
        _mod_kernel,
        out_shape=jax.ShapeDtypeStruct((bsz, n), F32),
        grid=(n // tn,),
        in_specs=[pl.BlockSpec((bsz, d), lambda j: (0, 0)),
                  pl.BlockSpec((d, tn), lambda j: (0, j)),
                  pl.BlockSpec((1, tn), lambda j: (0, j))],
        out_specs=pl.BlockSpec((bsz, tn), lambda j: (0, j)),
        compiler_params=_params("arbitrary"),
        name="mod",
    )(c, w_mod, b_mod)


def _inproj_kernel(x_ref, mod_ref, w_ref, u_ref, q_ref, k_ref, v_ref, gs_ref, ga_ref, *, d, q_scale):
    h = _layernorm(x_ref[...])
    h = (h * (1.0 + mod_ref[:, d:2 * d]) + mod_ref[:, 0:d]).astype(BF16)

    def proj(n):
        return jnp.dot(h, w_ref[:, n * d:(n + 1) * d], preferred_element_type=F32)

    u_ref[...] = proj(0).astype(BF16)
    q_ref[...] = (proj(1) * q_scale).astype(BF16)
    k_ref[...] = proj(2).astype(BF16)
    v_ref[...] = proj(3).astype(BF16)
    gs_ref[...] = jax.nn.sigmoid(proj(4)).astype(BF16)
    ga_ref[...] = jax.nn.sigmoid(proj(5)).astype(BF16)


def _inproj(x, mod3, w_in, tm):
    bsz, seq, d = x.shape
    row = pl.BlockSpec((None, tm, d), lambda b, t: (b, t, 0))
    out_row = jax.ShapeDtypeStruct((bsz, seq, d), BF16)
    return pl.pallas_call(
        functools.partial(_inproj_kernel, d=d, q_scale=ATT_HEAD_DIM ** -0.5 * math.log2(math.e)),
        out_shape=(jax.ShapeDtypeStruct((seq, bsz * d), BF16),) + (out_row,) * 5,
        grid=(bsz, seq // tm),
        in_specs=[row,
                  pl.BlockSpec((None, 1, mod3.shape[2]), lambda b, t: (b, 0, 0)),
                  _const_spec(w_in.shape)],
        out_specs=(pl.BlockSpec((tm, d), lambda b, t: (t, b)),) + (row,) * 5,
        compiler_params=_params("parallel", "parallel"),
        name="inproj",
    )(x, mod3, w_in)


def _ssm_kernel(u_ref, bblk_ref, cblk_ref, are_ref, aim_ref, d_ref, wglu_ref, bglu_ref, wbr_ref,
                out_ref, state_scr, bu_scr, sbf_scr, y_scr, *, ts, bsz, d):
    rows = ts * bsz
    n_tiles = d // GROUP_TILE
    st = GROUP_TILE * SSM_STATE // SSM_GROUP

    @pl.when(pl.program_id(0) == 0)
    def _():
        state_scr[...] = jnp.zeros_like(state_scr)

    u = u_ref[...].reshape(rows, d)
    for j in range(n_tiles):
        buf = j % 2
        bu_scr[buf] = jnp.dot(u[:, j * GROUP_TILE:(j + 1) * GROUP_TILE], bblk_ref[j],
                              preferred_element_type=F32)
        for c in range(st // SCAN_LANES):
            re = slice(c * SCAN_LANES, (c + 1) * SCAN_LANES)
            im = slice(st + c * SCAN_LANES, st + (c + 1) * SCAN_LANES)
            ar = jnp.broadcast_to(are_ref[j, :, re], (bsz, SCAN_LANES))
            ai = jnp.broadcast_to(aim_ref[j, :, re], (bsz, SCAN_LANES))
            sr = state_scr[j, :, re]
            si = state_scr[j, :, im]
            for t in range(ts):
                r = slice(t * bsz, (t + 1) * bsz)
                sr, si = (ar * sr - ai * si + bu_scr[buf, r, re],
                          ar * si + ai * sr + bu_scr[buf, r, im])
                sbf_scr[buf, r, re] = sr.astype(BF16)
                sbf_scr[buf, r, im] = si.astype(BF16)
            state_scr[j, :, re] = sr
            state_scr[j, :, im] = si
        y_scr[:, j * GROUP_TILE:(j + 1) * GROUP_TILE] = jnp.dot(
            sbf_scr[buf], cblk_ref[j], preferred_element_type=F32)

    y = y_scr[...] + d_ref[...] * u.astype(F32)
    z = jax.nn.gelu(y)
    gate = jax.nn.sigmoid(jnp.dot(z.astype(BF16), wglu_ref[...], preferred_element_type=F32) + bglu_ref[...])
    ys = (z * gate).astype(BF16)
    out = jnp.dot(ys, wbr_ref[...], preferred_element_type=F32)
    out_ref[...] = out.astype(BF16).reshape(ts, bsz, d)


def _ssm(u_t, bblk, cblk, a_re, a_im, d_skip, w_glu, b_glu, w_br, ts):
    seq, bsz, d = u_t.shape
    rows = ts * bsz
    st2 = bblk.shape[2]
    blk = pl.BlockSpec((ts, bsz, d), lambda i: (i, 0, 0))
    return pl.pallas_call(
        functools.partial(_ssm_kernel, ts=ts, bsz=bsz, d=d),
        out_shape=jax.ShapeDtypeStruct((seq, bsz, d), BF16),
        grid=(seq // ts,),
        in_specs=[blk, _const_spec(bblk.shape), _const_spec(cblk.shape), _const_spec(a_re.shape),
                  _const_spec(a_im.shape), _const_spec(d_skip.shape), _const_spec(w_glu.shape),
                  _const_spec(b_glu.shape), _const_spec(w_br.shape)],
        out_specs=blk,
        scratch_shapes=[pltpu.VMEM((bblk.shape[0], bsz, st2), F32),
                        pltpu.VMEM((2, rows, st2), F32),
                        pltpu.VMEM((2, rows, st2), BF16),
                        pltpu.VMEM((rows, d), F32)],
        compiler_params=_params("arbitrary"),
        name="ssm",
    )(u_t, bblk, cblk, a_re, a_im, d_skip, w_glu, b_glu, w_br)


def _ssm_tables(lam_re, lam_im, log_dt, b_re, b_im, c_re, c_im):
    g, p = lam_re.shape
    gpt = GROUP_TILE // SSM_GROUP
    nt = g // gpt
    dt = jnp.exp(log_dt)[:, None]
    mag = jnp.exp(lam_re * dt)
    ab_re = mag * jnp.cos(lam_im * dt)
    ab_im = mag * jnp.sin(lam_im * dt)
    den = lam_re * lam_re + lam_im * lam_im
    nr = ab_re - 1.0
    ni = ab_im
    coef_re = (nr * lam_re + ni * lam_im) / den
    coef_im = (ni * lam_re - nr * lam_im) / den
    bb_re = coef_re[..., None] * b_re - coef_im[..., None] * b_im
    bb_im = coef_re[..., None] * b_im + coef_im[..., None] * b_re
    eye = jnp.eye(gpt, dtype=F32)

    def in_blk(bb):
        t = bb.reshape(nt, gpt, p, SSM_GROUP)
        return jnp.einsum('jgpc,gh->jgchp', t, eye).reshape(nt, gpt * SSM_GROUP, gpt * p)

    def out_blk(cc):
        t = cc.reshape(nt, gpt, SSM_GROUP, p)
        return jnp.einsum('jgcp,gh->jgphc', t, eye).reshape(nt, gpt * p, gpt * SSM_GROUP)

    bblk = jnp.concatenate([in_blk(bb_re), in_blk(bb_im)], axis=2).astype(BF16)
    cblk = jnp.concatenate([out_blk(c_re), -out_blk(c_im)], axis=1).astype(BF16)
    a_re = ab_re.reshape(nt, 1, gpt * p)
    a_im = ab_im.reshape(nt, 1, gpt * p)
    return bblk, cblk, a_re, a_im


def _attn_kernel(lam_ref, q_ref, k_ref, v_ref, g_ref, o_ref, vext_scr, sa_scr, sb_scr, m_scr, acc_scr,
                 *, tq, nq, lambda_init):
    vd = ATT_V_DIM
    vext_scr[:, :vd] = v_ref[...]
    vext_scr[:, vd:] = jnp.ones((vext_scr.shape[0], vext_scr.shape[1] - vd), vext_scr.dtype)
    lane = lax.broadcasted_iota(jnp.int32, (tq, q_ref.shape[1]), 1)

    def start_tile():
        m_scr[...] = jnp.full_like(m_scr, NEG_BIG)
        acc_scr[...] = jnp.zeros_like(acc_scr)

    def scores(qi, j):
        q = q_ref[pl.ds(pl.multiple_of(qi * tq, tq), tq), :]
        zero = jnp.zeros_like(q)
        qs = jnp.concatenate([jnp.where(lane < ATT_HEAD_DIM, q, zero),
                              jnp.where(lane >= ATT_HEAD_DIM, q, zero)], axis=0)
        return lax.dot_general(qs, k_ref[pl.ds(pl.multiple_of(j * tq, tq), tq), :],
                               (((1,), (1,)), ((), ())), preferred_element_type=F32)

    def update(s, j):
        m_prev = m_scr[...]
        m_new = jnp.maximum(m_prev, jnp.max(s, axis=1, keepdims=True))
        p = jnp.exp2(s - jnp.tile(m_new, (1, tq // LANES)))
        alpha = jnp.exp2(m_prev - m_new)
        pv = jnp.dot(p.astype(BF16), vext_scr[pl.ds(pl.multiple_of(j * tq, tq), tq), :],
                     preferred_element_type=F32)
        acc_scr[...] = jnp.tile(alpha, (1, acc_scr.shape[1] // LANES)) * acc_scr[...] + pv
        m_scr[...] = m_new

    def masked(s):
        row = lax.broadcasted_iota(jnp.int32, (2 * tq, 1), 0)
        limit = ((row % tq) // CHUNK + 1) * CHUNK
        col = lax.broadcasted_iota(jnp.int32, s.shape, 1)
        return jnp.where(col < limit, s, NEG_BIG)

    def finish_tile(qi):
        acc = acc_scr[...]
        o = acc[:, :vd] / acc[:, vd:2 * vd]
        o = o[:tq] - lam_ref[0] * o[tq:]
        o = o * lax.rsqrt(jnp.mean(o * o, axis=-1, keepdims=True) + RMS_EPS)
        o_ref[pl.ds(pl.multiple_of(qi * tq, tq), tq), :] = (
            (o * g_ref[...]) * (1.0 - lambda_init)).astype(o_ref.dtype)
        start_tile()

    def successor(qi, j):
        last = j == qi
        return (jnp.where(last, jnp.minimum(qi + 1, nq - 1), qi), jnp.where(last, 0, j + 1))

    def pair(_, carry):
        qi1, j1 = carry
        qi2, j2 = successor(qi1, j1)
        qi3, j3 = successor(qi2, j2)
        diag1 = j1 == qi1
        diag2 = j2 == qi2

        def run(first_diag, second_diag):
            sb_scr[...] = scores(qi2, j2)
            s = sa_scr[...]
            update(masked(s) if first_diag else s, j1)
            if first_diag:
                finish_tile(qi1)
            sa_scr[...] = scores(qi3, j3)
            s = sb_scr[...]
            update(masked(s) if second_diag else s, j2)
            if second_diag:
                finish_tile(qi2)

        pl.when(jnp.logical_and(diag1, jnp.logical_not(diag2)))(lambda: run(True, False))
        pl.when(jnp.logical_and(jnp.logical_not(diag1), diag2))(lambda: run(False, True))
        pl.when(jnp.logical_and(jnp.logical_not(diag1), jnp.logical_not(diag2)))(lambda: run(False, False))
        return qi3, j3

    start_tile()
    sa_scr[...] = scores(0, 0)
    n_blocks = nq * (nq + 1) // 2
    lax.fori_loop(0, n_blocks // 2, pair, (jnp.int32(0), jnp.int32(0)))


def _attention(lam, q, k, v, subln_g, tq, lambda_init):
    bsz, seq, w = q.shape
    hd = 2 * ATT_HEAD_DIM
    nq = seq // tq
    assert nq >= 2 and (nq * (nq + 1) // 2) % 2 == 0, "blocks are walked in pairs"
    head = pl.BlockSpec((None, seq, hd), lambda b, h: (b, 0, h))
    return pl.pallas_call(
        functools.partial(_attn_kernel, tq=tq, nq=nq, lambda_init=lambda_init),
        out_shape=jax.ShapeDtypeStruct((bsz, seq, w), BF16),
        grid=(bsz, ATT_HEADS),
        in_specs=[pl.BlockSpec(memory_space=pltpu.SMEM), head, head, head,
                  pl.BlockSpec((1, ATT_V_DIM), lambda b, h: (0, 0))],
        out_specs=head,
        scratch_shapes=[pltpu.VMEM((seq, MXU_WIDTH), BF16),
                        pltpu.VMEM((2 * tq, tq), F32), pltpu.VMEM((2 * tq, tq), F32),
                        pltpu.VMEM((2 * tq, LANES), F32),
                        pltpu.VMEM((2 * tq, MXU_WIDTH), F32)],
        compiler_params=_params("parallel", "parallel"),
        name="attn",
    )(lam, q, k, v, subln_g)


def _merge_kernel(x_ref, ys_ref, ya_ref, gs_ref, ga_ref, mod_ref, wba_ref, wout_ref, g_ref, b_ref,
                  o_ref, *, d):
    att = jnp.dot(ya_ref[...], wba_ref[...], preferred_element_type=F32)
    merged = gs_ref[...].astype(F32) * ys_ref[...].astype(F32) + ga_ref[...].astype(F32) * att
    out = jnp.dot(merged.astype(BF16), wout_ref[...], preferred_element_type=F32)
    g1 = mod_ref[:, 2 * d:3 * d]
    o_ref[...] = _layernorm(DEEPNORM_ALPHA * x_ref[...] + g1 * out) * g_ref[...] + b_ref[...]


def _merge(x, ys2d, ya, gs, ga, mod3, w_ba, w_out, ln_g, ln_b, tm):
    bsz, seq, d = x.shape
    row = pl.BlockSpec((None, tm, d), lambda b, t: (b, t, 0))
    return pl.pallas_call(
        functools.partial(_merge_kernel, d=d),
        out_shape=jax.ShapeDtypeStruct((bsz, seq, d), F32),
        grid=(bsz, seq // tm),
        in_specs=[row, pl.BlockSpec((tm, d), lambda b, t: (t, b)), row, row, row,
                  pl.BlockSpec((None, 1, mod3.shape[2]), lambda b, t: (b, 0, 0)),
                  _const_spec(w_ba.shape), _const_spec(w_out.shape),
                  _const_spec(ln_g.shape), _const_spec(ln_b.shape)],
        out_specs=row,
        compiler_params=_params("parallel", "parallel"),
        name="merge",
    )(x, ys2d, ya, gs, ga, mod3, w_ba, w_out, ln_g, ln_b)


def _ffn_kernel(x_ref, mod_ref, wup_ref, cw_ref, cb_ref, wdn_ref, g_ref, b_ref, o_ref, carry_scr,
                *, d, dff, n_chunks):
    tm = x_ref.shape[0]
    tc = dff // n_chunks

    @pl.when(pl.program_id(1) == 0)
    def _():
        carry_scr[...] = jnp.zeros_like(carry_scr)

    x = x_ref[...]
    h = (_layernorm(x) * (1.0 + mod_ref[:, 4 * d:5 * d]) + mod_ref[:, 3 * d:4 * d]).astype(BF16)
    row = lax.broadcasted_iota(jnp.int32, (tm, tc), 0)
    f = jnp.zeros((tm, d), F32)
    for c in range(n_chunks):
        cols = slice(c * tc, (c + 1) * tc)
        a = jnp.dot(h, wup_ref[:, cols], preferred_element_type=F32)
        val = jnp.dot(h, wup_ref[:, dff + c * tc:dff + (c + 1) * tc], preferred_element_type=F32)
        prev = carry_scr[:, cols]
        p1 = jnp.broadcast_to(prev[7:8], (tm, tc))
        p2 = jnp.broadcast_to(prev[6:7], (tm, tc))
        a1 = jnp.where(row == 0, p1, pltpu.roll(a, 1, 0))
        a2 = jnp.where(row == 0, p2, jnp.where(row == 1, p1, pltpu.roll(a, 2, 0)))
        carry_scr[:, cols] = a[tm - 8:tm]
        ac = cw_ref[0:1, cols] * a2 + cw_ref[1:2, cols] * a1 + cw_ref[2:3, cols] * a + cb_ref[:, cols]
        gl = (jax.nn.silu(ac) * val).astype(BF16)
        f = f + jnp.dot(gl, wdn_ref[cols, :], preferred_element_type=F32)
    g2 = mod_ref[:, 5 * d:6 * d]
    o_ref[...] = _layernorm(DEEPNORM_ALPHA * x + g2 * f) * g_ref[...] + b_ref[...]


def _ffn(x, mod3, w_up, conv_w, conv_b, w_down, ln_g, ln_b, tm):
    bsz, seq, d = x.shape
    dff = w_down.shape[0]
    row = pl.BlockSpec((None, tm, d), lambda b, t: (b, t, 0))
    return pl.pallas_call(
        functools.partial(_ffn_kernel, d=d, dff=dff, n_chunks=2),
        out_shape=jax.ShapeDtypeStruct((bsz, seq, d), F32),
        grid=(bsz, seq // tm),
        in_specs=[row, pl.BlockSpec((None, 1, mod3.shape[2]), lambda b, t: (b, 0, 0)),
                  _const_spec(w_up.shape), _const_spec(conv_w.shape), _const_spec(conv_b.shape),
                  _const_spec(w_down.shape), _const_spec(ln_g.shape), _const_spec(ln_b.shape)],
        out_specs=row,
        scratch_shapes=[pltpu.VMEM((8, dff), F32)],
        compiler_params=_params("parallel", "arbitrary"),
        name="ffn",
    )(x, mod3, w_up, conv_w, conv_b, w_down, ln_g, ln_b)


def kernel(x, c, w_mod, b_mod, w_in, ssm_lambda_re, ssm_lambda_im, ssm_log_dt, ssm_b_re, ssm_b_im, ssm_c_re, ssm_c_im, ssm_d, ssm_w_glu, ssm_b_glu, att_lambda_q1, att_lambda_k1, att_lambda_q2, att_lambda_k2, att_subln_g, w_branch_ssm, w_branch_att, w_out, ln1_g, ln1_b, w_up, conv_w, conv_b, w_down, ln2_g, ln2_b):
    bsz, seq, d = x.shape
    depth = w_mod.shape[0]
    assert depth == DEPTH
    tm = min(256, seq)
    ts = min(32, seq)
    tq = min(512, seq)
    for l in range(depth):
        mod3 = _mod(c, w_mod[l].astype(BF16), b_mod[l][None, :])[:, None, :]
        u2d, q, k, v, gs, ga = _inproj(x, mod3, w_in[l].astype(BF16), tm)

        bblk, cblk, a_re, a_im = _ssm_tables(ssm_lambda_re[l], ssm_lambda_im[l], ssm_log_dt[l],
                                             ssm_b_re[l], ssm_b_im[l], ssm_c_re[l], ssm_c_im[l])
        ys_t = _ssm(u2d.reshape(seq, bsz, d), bblk, cblk, a_re, a_im, ssm_d[l][None, :],
                    ssm_w_glu[l].astype(BF16), ssm_b_glu[l][None, :], w_branch_ssm[l].astype(BF16), ts)

        lam = (jnp.exp(jnp.sum(att_lambda_q1[l] * att_lambda_k1[l]))
               - jnp.exp(jnp.sum(att_lambda_q2[l] * att_lambda_k2[l])) + _lambda_init(l))
        ya = _attention(lam.reshape(1).astype(F32), q, k, v, att_subln_g[l][None, :], tq,
                        _lambda_init(l))

        x = _merge(x, ys_t.reshape(seq, bsz * d), ya, gs, ga, mod3, w_branch_att[l].astype(BF16),
                   w_out[l].astype(BF16), ln1_g[l][None, :], ln1_b[l][None, :], tm)
        x = _ffn(x, mod3, w_up[l].astype(BF16), conv_w[l].reshape(CONV_W, -1), conv_b[l][None, :],
                 w_down[l].astype(BF16), ln2_g[l][None, :], ln2_b[l][None, :], tm)
    return x
```

```python
import functools
import math

import jax
import jax.numpy as jnp
from jax import lax
from jax.experimental import pallas as pl
from jax.experimental.pallas import tpu as pltpu

F32 = jnp.float32
BF16 = jnp.bfloat16

CHUNK = 64
ATT_HEADS = 8
ATT_HEAD_DIM = 64
ATT_V_DIM = 2 * ATT_HEAD_DIM
SSM_GROUP = 16
SSM_STATE = 64
CONV_W = 3
DEPTH = 1
DEEPNORM_ALPHA = (2.0 * DEPTH) ** 0.25
LN_EPS = 1e-5
RMS_EPS = 1e-5

def _lambda_init(layer):
    return 0.8 - 0.6 * math.exp(-0.3 * layer)


LANES = 128
MXU_WIDTH = 256
VMEM_LIMIT_BYTES = 56 * 1024 * 1024

GROUP_TILE = 256
SCAN_LANES = 256
INPROJ_ROW_GROUP = 256
MIX_ROW_GROUP = 512
ROW_TILE = 512
SCAN_STEPS = 32
ATT_TILE = 512
NEG_BIG = -1e30


def _layernorm(x):
    mu = jnp.mean(x, axis=-1, keepdims=True)
    xc = x - mu
    var = jnp.mean(xc * xc, axis=-1, keepdims=True)
    return xc * lax.rsqrt(var + LN_EPS)


def _row_groups(tm, group):
    g = min(group, tm)
    return [slice(r, r + g) for r in range(0, tm, g)]


def _const_spec(shape):
    nd = len(shape)
    return pl.BlockSpec(shape, lambda *_: (0,) * nd, pipeline_mode=pl.Buffered(1))


def _params(*sem):
    return pltpu.CompilerParams(dimension_semantics=sem, vmem_limit_bytes=VMEM_LIMIT_BYTES)


def _mod_kernel(c_ref, w_ref, b_ref, o_ref):
    cond = jax.nn.silu(c_ref[...]).astype(BF16)
    o_ref[...] = jnp.dot(cond, w_ref[...], preferred_element_type=F32) + b_ref[...]


def _mod(c, w_mod, b_mod):
    bsz, d = c.shape
    n = w_mod.shape[1]
    tn = d
    return pl.pallas_call(
        _mod_kernel,
        out_shape=jax.ShapeDtypeStruct((bsz, n), F32),
        grid=(n // tn,),
        in_specs=[pl.BlockSpec((bsz, d), lambda j: (0, 0)),
                  pl.BlockSpec((d, tn), lambda j: (0, j)),
                  pl.BlockSpec((1, tn), lambda j: (0, j))],
        out_specs=pl.BlockSpec((bsz, tn), lambda j: (0, j)),
        compiler_params=_params("arbitrary"),
        name="mod",
    )(c, w_mod, b_mod)


def _inproj_kernel(x_ref, mod_ref, w_ref, u_ref, q_ref, k_ref, v_ref, gs_ref, ga_ref, *, d, q_scale):
    for rows in _row_groups(x_ref.shape[0], INPROJ_ROW_GROUP):
        h = _layernorm(x_ref[rows, :])
        h = (h * (1.0 + mod_ref[:, d:2 * d]) + mod_ref[:, 0:d]).astype(BF16)

        def proj(n, h=h):
            return jnp.dot(h, w_ref[:, n * d:(n + 1) * d], preferred_element_type=F32)

        u_ref[rows, :] = proj(0).astype(BF16)
        q_ref[rows, :] = (proj(1) * q_scale).astype(BF16)
        k_ref[rows, :] = proj(2).astype(BF16)
        v_ref[rows, :] = proj(3).astype(BF16)
        gs_ref[rows, :] = jax.nn.sigmoid(proj(4)).astype(BF16)
        ga_ref[rows, :] = jax.nn.sigmoid(proj(5)).astype(BF16)


def _inproj(x, mod3, w_in, tm):
    bsz, seq, d = x.shape
    row = pl.BlockSpec((None, tm, d), lambda b, t: (b, t, 0))
    out_row = jax.ShapeDtypeStruct((bsz, seq, d), BF16)
    return pl.pallas_call(
        functools.partial(_inproj_kernel, d=d, q_scale=ATT_HEAD_DIM ** -0.5 * math.log2(math.e)),
        out_shape=(jax.ShapeDtypeStruct((seq, bsz * d), BF16),) + (out_row,) * 5,
        grid=(bsz, seq // tm),
        in_specs=[row,
                  pl.BlockSpec((None, 1, mod3.shape[2]), lambda b, t: (b, 0, 0)),
                  _const_spec(w_in.shape)],
        out_specs=(pl.BlockSpec((tm, d), lambda b, t: (t, b)),) + (row,) * 5,
        compiler_params=_params("parallel", "parallel"),
        name="inproj",
    )(x, mod3, w_in)


def _ssm_kernel(u_ref, bblk_ref, cblk_ref, are_ref, aim_ref, d_ref, wglu_ref, bglu_ref, wbr_ref,
                out_ref, state_scr, bu_scr, sbf_scr, y_scr, *, ts, bsz, d):
    rows = ts * bsz
    n_tiles = d // GROUP_TILE
    st = GROUP_TILE * SSM_STATE // SSM_GROUP

    @pl.when(pl.program_id(0) == 0)
    def _():
        state_scr[...] = jnp.zeros_like(state_scr)

    u = u_ref[...].reshape(rows, d)
    for j in range(n_tiles):
        buf = j % 2
        bu_scr[buf] = jnp.dot(u[:, j * GROUP_TILE:(j + 1) * GROUP_TILE], bblk_ref[j],
                              preferred_element_type=F32)
        for c in range(st // SCAN_LANES):
            re = slice(c * SCAN_LANES, (c + 1) * SCAN_LANES)
            im = slice(st + c * SCAN_LANES, st + (c + 1) * SCAN_LANES)
            ar = jnp.broadcast_to(are_ref[j, :, re], (bsz, SCAN_LANES))
            ai = jnp.broadcast_to(aim_ref[j, :, re], (bsz, SCAN_LANES))
            sr = state_scr[j, :, re]
            si = state_scr[j, :, im]
            for t in range(ts):
                r = slice(t * bsz, (t + 1) * bsz)
                sr, si = (ar * sr - ai * si + bu_scr[buf, r, re],
                          ar * si + ai * sr + bu_scr[buf, r, im])
                sbf_scr[buf, r, re] = sr.astype(BF16)
                sbf_scr[buf, r, im] = si.astype(BF16)
            state_scr[j, :, re] = sr
            state_scr[j, :, im] = si
        y_scr[:, j * GROUP_TILE:(j + 1) * GROUP_TILE] = jnp.dot(
            sbf_scr[buf], cblk_ref[j], preferred_element_type=F32)

    y = y_scr[...] + d_ref[...] * u.astype(F32)
    z = jax.nn.gelu(y)
    gate = jax.nn.sigmoid(jnp.dot(z.astype(BF16), wglu_ref[...], preferred_element_type=F32) + bglu_ref[...])
    ys = (z * gate).astype(BF16)
    out = jnp.dot(ys, wbr_ref[...], preferred_element_type=F32)
    out_ref[...] = out.astype(BF16).reshape(ts, bsz, d)


def _ssm(u_t, bblk, cblk, a_re, a_im, d_skip, w_glu, b_glu, w_br, ts):
    seq, bsz, d = u_t.shape
    rows = ts * bsz
    st2 = bblk.shape[2]
    blk = pl.BlockSpec((ts, bsz, d), lambda i: (i, 0, 0))
    return pl.pallas_call(
        functools.partial(_ssm_kernel, ts=ts, bsz=bsz, d=d),
        out_shape=jax.ShapeDtypeStruct((seq, bsz, d), BF16),
        grid=(seq // ts,),
        in_specs=[blk, _const_spec(bblk.shape), _const_spec(cblk.shape), _const_spec(a_re.shape),
                  _const_spec(a_im.shape), _const_spec(d_skip.shape), _const_spec(w_glu.shape),
                  _const_spec(b_glu.shape), _const_spec(w_br.shape)],
        out_specs=blk,
        scratch_shapes=[pltpu.VMEM((bblk.shape[0], bsz, st2), F32),
                        pltpu.VMEM((2, rows, st2), F32),
                        pltpu.VMEM((2, rows, st2), BF16),
                        pltpu.VMEM((rows, d), F32)],
        compiler_params=_params("arbitrary"),
        name="ssm",
    )(u_t, bblk, cblk, a_re, a_im, d_skip, w_glu, b_glu, w_br)


def _ssm_tables(lam_re, lam_im, log_dt, b_re, b_im, c_re, c_im):
    g, p = lam_re.shape
    gpt = GROUP_TILE // SSM_GROUP
    nt = g // gpt
    dt = jnp.exp(log_dt)[:, None]
    mag = jnp.exp(lam_re * dt)
    ab_re = mag * jnp.cos(lam_im * dt)
    ab_im = mag * jnp.sin(lam_im * dt)
    den = lam_re * lam_re + lam_im * lam_im
    nr = ab_re - 1.0
    ni = ab_im
    coef_re = (nr * lam_re + ni * lam_im) / den
    coef_im = (ni * lam_re - nr * lam_im) / den
    bb_re = coef_re[..., None] * b_re - coef_im[..., None] * b_im
    bb_im = coef_re[..., None] * b_im + coef_im[..., None] * b_re
    eye = jnp.eye(gpt, dtype=F32)

    def in_blk(bb):
        t = bb.reshape(nt, gpt, p, SSM_GROUP)
        return jnp.einsum('jgpc,gh->jgchp', t, eye).reshape(nt, gpt * SSM_GROUP, gpt * p)

    def out_blk(cc):
        t = cc.reshape(nt, gpt, SSM_GROUP, p)
        return jnp.einsum('jgcp,gh->jgphc', t, eye).reshape(nt, gpt * p, gpt * SSM_GROUP)

    bblk = jnp.concatenate([in_blk(bb_re), in_blk(bb_im)], axis=2).astype(BF16)
    cblk = jnp.concatenate([out_blk(c_re), -out_blk(c_im)], axis=1).astype(BF16)
    a_re = ab_re.reshape(nt, 1, gpt * p)
    a_im = ab_im.reshape(nt, 1, gpt * p)
    return bblk, cblk, a_re, a_im


def _attn_kernel(lam_ref, q_ref, k_ref, v_ref, g_ref, o_ref, vext_scr, sa_scr, sb_scr, m_scr, acc_scr,
                 *, tq, nq, group, lambda_init):
    vd = ATT_V_DIM
    vext_scr[:, :vd] = v_ref[...]
    vext_scr[:, vd:] = jnp.ones((vext_scr.shape[0], vext_scr.shape[1] - vd), vext_scr.dtype)
    lane = lax.broadcasted_iota(jnp.int32, (tq, q_ref.shape[1]), 1)
    bufs = (sa_scr, sb_scr)

    def scores(qi, j):
        q = q_ref[pl.ds(pl.multiple_of(qi * tq, tq), tq), :]
        zero = jnp.zeros_like(q)
        qs = jnp.concatenate([jnp.where(lane < ATT_HEAD_DIM, q, zero),
                              jnp.where(lane >= ATT_HEAD_DIM, q, zero)], axis=0)
        return lax.dot_general(qs, k_ref[pl.ds(pl.multiple_of(j * tq, tq), tq), :],
                               (((1,), (1,)), ((), ())), preferred_element_type=F32)

    def values(p, j):
        return jnp.dot(p.astype(BF16), vext_scr[pl.ds(pl.multiple_of(j * tq, tq), tq), :],
                       preferred_element_type=F32)

    def first_block(s, qi):
        row = lax.broadcasted_iota(jnp.int32, (2 * tq, 1), 0)
        limit = ((row % tq) // CHUNK + 1) * CHUNK
        col = lax.broadcasted_iota(jnp.int32, s.shape, 1)
        s = jnp.where(col < limit, s, NEG_BIG)
        m = jnp.broadcast_to(jnp.max(s, axis=1, keepdims=True), (2 * tq, LANES))
        acc_scr[qi] = values(jnp.exp2(s - jnp.tile(m, (1, tq // LANES))), qi)
        m_scr[qi] = m

    def next_block(s, qi, j):
        m_prev = m_scr[qi]
        m_new = jnp.maximum(m_prev, jnp.max(s, axis=1, keepdims=True))
        alpha = jnp.exp2(m_prev - m_new)
        pv = values(jnp.exp2(s - jnp.tile(m_new, (1, tq // LANES))), j)
        acc_scr[qi] = jnp.tile(alpha, (1, acc_scr.shape[2] // LANES)) * acc_scr[qi] + pv
        m_scr[qi] = m_new

    def successor(qi, j):
        last = j + 1 == qi
        return (jnp.where(last, jnp.minimum(qi + 1, nq - 1), qi), jnp.where(last, 0, j + 1))

    def diagonal_pair(i, carry):
        qa = 2 * i
        sb_scr[...] = scores(qa + 1, qa + 1)
        first_block(sa_scr[...], qa)
        more = qa + 2 < nq
        sa_scr[...] = scores(jnp.where(more, qa + 2, 1), jnp.where(more, qa + 2, 0))
        first_block(sb_scr[...], qa + 1)
        return carry

    def full_group(_, carry):
        qi, j = carry
        for g in range(group):
            nqi, nj = successor(qi, j)
            bufs[(g + 1) % 2][...] = scores(nqi, nj)
            next_block(bufs[g % 2][...], qi, j)
            qi, j = nqi, nj
        return qi, j

    def normalise(qi, carry):
        acc = acc_scr[qi]
        o = acc[:, :vd] / acc[:, vd:2 * vd]
        o = o[:tq] - lam_ref[0] * o[tq:]
        o = o * lax.rsqrt(jnp.mean(o * o, axis=-1, keepdims=True) + RMS_EPS)
        o_ref[pl.ds(pl.multiple_of(qi * tq, tq), tq), :] = (
            (o * g_ref[...]) * (1.0 - lambda_init)).astype(o_ref.dtype)
        return carry

    sa_scr[...] = scores(0, 0)
    lax.fori_loop(0, nq // 2, diagonal_pair, 0)
    lax.fori_loop(0, nq * (nq - 1) // 2 // group, full_group, (jnp.int32(1), jnp.int32(0)))
    lax.fori_loop(0, nq, normalise, 0)


def _attention(lam, q, k, v, subln_g, tq, lambda_init):
    bsz, seq, w = q.shape
    hd = 2 * ATT_HEAD_DIM
    nq = seq // tq
    n_full = nq * (nq - 1) // 2
    group = 4 if n_full % 4 == 0 else 2
    assert nq % 2 == 0 and n_full % group == 0, "diagonal blocks go in pairs, full blocks in even groups"
    head = pl.BlockSpec((None, seq, hd), lambda b, h: (b, 0, h))
    return pl.pallas_call(
        functools.partial(_attn_kernel, tq=tq, nq=nq, group=group, lambda_init=lambda_init),
        out_shape=jax.ShapeDtypeStruct((bsz, seq, w), BF16),
        grid=(bsz, ATT_HEADS),
        in_specs=[pl.BlockSpec(memory_space=pltpu.SMEM), head, head, head,
                  pl.BlockSpec((1, ATT_V_DIM), lambda b, h: (0, 0))],
        out_specs=head,
        scratch_shapes=[pltpu.VMEM((seq, MXU_WIDTH), BF16),
                        pltpu.VMEM((2 * tq, tq), F32), pltpu.VMEM((2 * tq, tq), F32),
                        pltpu.VMEM((nq, 2 * tq, LANES), F32),
                        pltpu.VMEM((nq, 2 * tq, MXU_WIDTH), F32)],
        compiler_params=_params("parallel", "parallel"),
        name="attn",
    )(lam, q, k, v, subln_g)


def _ffn_chunks(dff):
    tiles = dff // MXU_WIDTH
    cut = (tiles + 1) // 2 * MXU_WIDTH
    return [(0, cut), (cut, dff)] if dff % MXU_WIDTH == 0 and tiles > 1 else [(0, dff)]


def _mix_kernel(x_ref, ys_ref, ya_ref, gs_ref, ga_ref, mod_ref, wba_ref, wout_ref, g1_ref, b1_ref,
                wup_ref, cw_ref, cb_ref, wdn_ref, g2_ref, b2_ref, o_ref, carry_scr, *, d, dff):
    @pl.when(pl.program_id(1) == 0)
    def _():
        carry_scr[...] = jnp.zeros_like(carry_scr)

    for rows in _row_groups(x_ref.shape[0], MIX_ROW_GROUP):
        n = rows.stop - rows.start
        att = jnp.dot(ya_ref[rows, :], wba_ref[...], preferred_element_type=F32)
        merged = (gs_ref[rows, :].astype(F32) * ys_ref[rows, :].astype(F32)
                  + ga_ref[rows, :].astype(F32) * att)
        out = jnp.dot(merged.astype(BF16), wout_ref[...], preferred_element_type=F32)
        x1 = (_layernorm(DEEPNORM_ALPHA * x_ref[rows, :] + mod_ref[:, 2 * d:3 * d] * out) * g1_ref[...]
              + b1_ref[...])

        h = (_layernorm(x1) * (1.0 + mod_ref[:, 4 * d:5 * d]) + mod_ref[:, 3 * d:4 * d]).astype(BF16)
        f = jnp.zeros((n, d), F32)
        for c0, c1 in _ffn_chunks(dff):
            a = jnp.dot(h, wup_ref[:, c0:c1], preferred_element_type=F32)
            val = jnp.dot(h, wup_ref[:, dff + c0:dff + c1], preferred_element_type=F32)
            row = lax.broadcasted_iota(jnp.int32, a.shape, 0)
            prev = carry_scr[:, c0:c1]
            p1 = jnp.broadcast_to(prev[7:8], a.shape)
            p2 = jnp.broadcast_to(prev[6:7], a.shape)
            a1 = jnp.where(row == 0, p1, pltpu.roll(a, 1, 0))
            a2 = jnp.where(row == 0, p2, jnp.where(row == 1, p1, pltpu.roll(a, 2, 0)))
            carry_scr[:, c0:c1] = a[n - 8:n]
            ac = cw_ref[0:1, c0:c1] * a2 + cw_ref[1:2, c0:c1] * a1 + cw_ref[2:3, c0:c1] * a + cb_ref[:, c0:c1]
            gl = (jax.nn.silu(ac) * val).astype(BF16)
            f = f + jnp.dot(gl, wdn_ref[c0:c1, :], preferred_element_type=F32)
        o_ref[rows, :] = (_layernorm(DEEPNORM_ALPHA * x1 + mod_ref[:, 5 * d:6 * d] * f) * g2_ref[...]
                          + b2_ref[...])


def _mix(x, ys2d, ya, gs, ga, mod3, w_ba, w_out, ln1_g, ln1_b, w_up, conv_w, conv_b, w_down, ln2_g, ln2_b, tm):
    bsz, seq, d = x.shape
    dff = w_down.shape[0]
    row = pl.BlockSpec((None, tm, d), lambda b, t: (b, t, 0))
    consts = (w_ba, w_out, ln1_g, ln1_b, w_up, conv_w, conv_b, w_down, ln2_g, ln2_b)
    return pl.pallas_call(
        functools.partial(_mix_kernel, d=d, dff=dff),
        out_shape=jax.ShapeDtypeStruct((bsz, seq, d), F32),
        grid=(bsz, seq // tm),
        in_specs=[row, pl.BlockSpec((tm, d), lambda b, t: (t, b)), row, row, row,
                  pl.BlockSpec((None, 1, mod3.shape[2]), lambda b, t: (b, 0, 0))]
                 + [_const_spec(w.shape) for w in consts],
        out_specs=row,
        scratch_shapes=[pltpu.VMEM((8, dff), F32)],
        compiler_params=_params("parallel", "arbitrary"),
        name="mix",
    )(x, ys2d, ya, gs, ga, mod3, *consts)


def kernel(x, c, w_mod, b_mod, w_in, ssm_lambda_re, ssm_lambda_im, ssm_log_dt, ssm_b_re, ssm_b_im, ssm_c_re, ssm_c_im, ssm_d, ssm_w_glu, ssm_b_glu, att_lambda_q1, att_lambda_k1, att_lambda_q2, att_lambda_k2, att_subln_g, w_branch_ssm, w_branch_att, w_out, ln1_g, ln1_b, w_up, conv_w, conv_b, w_down, ln2_g, ln2_b):
    bsz, seq, d = x.shape
    depth = w_mod.shape[0]
    assert depth == DEPTH
    tm = min(ROW_TILE, seq)
    ts = min(SCAN_STEPS, seq)
    tq = min(ATT_TILE, seq)
    for l in range(depth):
        mod3 = _mod(c, w_mod[l].astype(BF16), b_mod[l][None, :])[:, None, :]
        u2d, q, k, v, gs, ga = _inproj(x, mod3, w_in[l].astype(BF16), tm)

        bblk, cblk, a_re, a_im = _ssm_tables(ssm_lambda_re[l], ssm_lambda_im[l], ssm_log_dt[l],
                                             ssm_b_re[l], ssm_b_im[l], ssm_c_re[l], ssm_c_im[l])
        ys_t = _ssm(u2d.reshape(seq, bsz, d), bblk, cblk, a_re, a_im, ssm_d[l][None, :],
                    ssm_w_glu[l].astype(BF16), ssm_b_glu[l][None, :], w_branch_ssm[l].astype(BF16), ts)

        lam = (jnp.exp(jnp.sum(att_lambda_q1[l] * att_lambda_k1[l]))
               - jnp.exp(jnp.sum(att_lambda_q2[l] * att_lambda_k2[l])) + _lambda_init(l))
        ya = _attention(lam.reshape(1).astype(F32), q, k, v, att_subln_g[l][None, :], tq,
                        _lambda_init(l))

        x = _mix(x, ys_t.reshape(seq, bsz * d), ya, gs, ga, mod3, w_branch_att[l].astype(BF16),
                 w_out[l].astype(BF16), ln1_g[l][None, :], ln1_b[l][None, :],
                 w_up[l].astype(BF16), conv_w[l].reshape(CONV_W, -1), conv_b[l][None, :],
                 w_down[l].astype(BF16), ln2_g[l][None, :], ln2_b[l][None, :], tm)
    return x
```

```python
import functools
import math

import jax
import jax.numpy as jnp
from jax import lax
from jax.experimental import pallas as pl
from jax.experimental.pallas import tpu as pltpu

F32 = jnp.float32
BF16 = jnp.bfloat16

CHUNK = 64
ATT_HEADS = 8
ATT_HEAD_DIM = 64
ATT_V_DIM = 2 * ATT_HEAD_DIM
SSM_GROUP = 16
SSM_STATE = 64
CONV_W = 3
DEPTH = 1
DEEPNORM_ALPHA = (2.0 * DEPTH) ** 0.25
LN_EPS = 1e-5
RMS_EPS = 1e-5

def _lambda_init(layer):
    return 0.8 - 0.6 * math.exp(-0.3 * layer)


LANES = 128
MXU_WIDTH = 256
VMEM_LIMIT_BYTES = 56 * 1024 * 1024

GROUP_TILE = 256
SCAN_LANES = 256
INPROJ_ROW_GROUP = 256
MIX_ROW_GROUP = 512
ROW_TILE = 512
SCAN_STEPS = 32
ATT_TILE = 512
ATT_GROUP_MAX = 14
NEG_BIG = -1e30


def _layernorm(x):
    mu = jnp.mean(x, axis=-1, keepdims=True)
    xc = x - mu
    var = jnp.mean(xc * xc, axis=-1, keepdims=True)
    return xc * lax.rsqrt(var + LN_EPS)


def _row_groups(tm, group):
    g = min(group, tm)
    return [slice(r, r + g) for r in range(0, tm, g)]


def _const_spec(shape):
    nd = len(shape)
    return pl.BlockSpec(shape, lambda *_: (0,) * nd, pipeline_mode=pl.Buffered(1))


def _params(*sem):
    return pltpu.CompilerParams(dimension_semantics=sem, vmem_limit_bytes=VMEM_LIMIT_BYTES)


def _mod_kernel(c_ref, w_ref, b_ref, o_ref):
    cond = jax.nn.silu(c_ref[...]).astype(BF16)
    o_ref[...] = jnp.dot(cond, w_ref[...], preferred_element_type=F32) + b_ref[...]


def _mod(c, w_mod, b_mod):
    bsz, d = c.shape
    n = w_mod.shape[1]
    tn = d
    return pl.pallas_call(
        _mod_kernel,
        out_shape=jax.ShapeDtypeStruct((bsz, n), F32),
        grid=(n // tn,),
        in_specs=[pl.BlockSpec((bsz, d), lambda j: (0, 0)),
                  pl.BlockSpec((d, tn), lambda j: (0, j)),
                  pl.BlockSpec((1, tn), lambda j: (0, j))],
        out_specs=pl.BlockSpec((bsz, tn), lambda j: (0, j)),
        compiler_params=_params("arbitrary"),
        name="mod",
    )(c, w_mod, b_mod)


def _inproj_kernel(x_ref, mod_ref, w_ref, u_ref, q_ref, k_ref, v_ref, gs_ref, ga_ref, *, d, q_scale):
    for rows in _row_groups(x_ref.shape[0], INPROJ_ROW_GROUP):
        h = _layernorm(x_ref[rows, :])
        h = (h * (1.0 + mod_ref[:, d:2 * d]) + mod_ref[:, 0:d]).astype(BF16)

        def proj(n, h=h):
            return jnp.dot(h, w_ref[:, n * d:(n + 1) * d], preferred_element_type=F32)

        u_ref[rows, :] = proj(0).astype(BF16)
        q_ref[rows, :] = (proj(1) * q_scale).astype(BF16)
        k_ref[rows, :] = proj(2).astype(BF16)
        v_ref[rows, :] = proj(3).astype(BF16)
        gs_ref[rows, :] = jax.nn.sigmoid(proj(4)).astype(BF16)
        ga_ref[rows, :] = jax.nn.sigmoid(proj(5)).astype(BF16)


def _inproj(x, mod3, w_in, tm):
    bsz, seq, d = x.shape
    row = pl.BlockSpec((None, tm, d), lambda b, t: (b, t, 0))
    out_row = jax.ShapeDtypeStruct((bsz, seq, d), BF16)
    return pl.pallas_call(
        functools.partial(_inproj_kernel, d=d, q_scale=ATT_HEAD_DIM ** -0.5 * math.log2(math.e)),
        out_shape=(jax.ShapeDtypeStruct((seq, bsz * d), BF16),) + (out_row,) * 5,
        grid=(bsz, seq // tm),
        in_specs=[row,
                  pl.BlockSpec((None, 1, mod3.shape[2]), lambda b, t: (b, 0, 0)),
                  _const_spec(w_in.shape)],
        out_specs=(pl.BlockSpec((tm, d), lambda b, t: (t, b)),) + (row,) * 5,
        compiler_params=_params("parallel", "parallel"),
        name="inproj",
    )(x, mod3, w_in)


def _ssm_kernel(u_ref, bblk_ref, cblk_ref, are_ref, aim_ref, d_ref, wglu_ref, bglu_ref, wbr_ref,
                out_ref, state_scr, bu_scr, sbf_scr, y_scr, *, ts, bsz, d):
    rows = ts * bsz
    n_tiles = d // GROUP_TILE
    st = GROUP_TILE * SSM_STATE // SSM_GROUP

    @pl.when(pl.program_id(0) == 0)
    def _():
        state_scr[...] = jnp.zeros_like(state_scr)

    u = u_ref[...].reshape(rows, d)
    for j in range(n_tiles):
        cols = slice(j * GROUP_TILE, (j + 1) * GROUP_TILE)
        buf = j % 2
        bu_scr[buf] = jnp.dot(u[:, cols], bblk_ref[j], preferred_element_type=F32)
        for c in range(st // SCAN_LANES):
            re = slice(c * SCAN_LANES, (c + 1) * SCAN_LANES)
            im = slice(st + c * SCAN_LANES, st + (c + 1) * SCAN_LANES)
            ar = jnp.broadcast_to(are_ref[j, :, re], (bsz, SCAN_LANES))
            ai = jnp.broadcast_to(aim_ref[j, :, re], (bsz, SCAN_LANES))
            sr = state_scr[j, :, re]
            si = state_scr[j, :, im]
            for t in range(ts):
                r = slice(t * bsz, (t + 1) * bsz)
                sr, si = (ar * sr - ai * si + bu_scr[buf, r, re],
                          ar * si + ai * sr + bu_scr[buf, r, im])
                sbf_scr[buf, r, re] = sr.astype(BF16)
                sbf_scr[buf, r, im] = si.astype(BF16)
            state_scr[j, :, re] = sr
            state_scr[j, :, im] = si
        y_scr[:, cols] = (jnp.dot(sbf_scr[buf], cblk_ref[j], preferred_element_type=F32)
                          + d_ref[:, cols] * u[:, cols].astype(F32))

    z = jax.nn.gelu(y_scr[...])
    gate = jax.nn.sigmoid(jnp.dot(z.astype(BF16), wglu_ref[...], preferred_element_type=F32) + bglu_ref[...])
    out = jnp.dot((z * gate).astype(BF16), wbr_ref[...], preferred_element_type=F32)
    out_ref[...] = out.astype(BF16).reshape(ts, bsz, d)


def _ssm(u_t, bblk, cblk, a_re, a_im, d_skip, w_glu, b_glu, w_br, ts):
    seq, bsz, d = u_t.shape
    rows = ts * bsz
    st2 = bblk.shape[2]
    blk = pl.BlockSpec((ts, bsz, d), lambda i: (i, 0, 0))
    return pl.pallas_call(
        functools.partial(_ssm_kernel, ts=ts, bsz=bsz, d=d),
        out_shape=jax.ShapeDtypeStruct((seq, bsz, d), BF16),
        grid=(seq // ts,),
        in_specs=[blk, _const_spec(bblk.shape), _const_spec(cblk.shape), _const_spec(a_re.shape),
                  _const_spec(a_im.shape), _const_spec(d_skip.shape), _const_spec(w_glu.shape),
                  _const_spec(b_glu.shape), _const_spec(w_br.shape)],
        out_specs=blk,
        scratch_shapes=[pltpu.VMEM((bblk.shape[0], bsz, st2), F32),
                        pltpu.VMEM((2, rows, st2), F32),
                        pltpu.VMEM((2, rows, st2), BF16),
                        pltpu.VMEM((rows, d), F32)],
        compiler_params=_params("arbitrary"),
        name="ssm",
    )(u_t, bblk, cblk, a_re, a_im, d_skip, w_glu, b_glu, w_br)


def _ssm_tables(lam_re, lam_im, log_dt, b_re, b_im, c_re, c_im):
    g, p = lam_re.shape
    gpt = GROUP_TILE // SSM_GROUP
    nt = g // gpt
    dt = jnp.exp(log_dt)[:, None]
    mag = jnp.exp(lam_re * dt)
    ab_re = mag * jnp.cos(lam_im * dt)
    ab_im = mag * jnp.sin(lam_im * dt)
    den = lam_re * lam_re + lam_im * lam_im
    nr = ab_re - 1.0
    ni = ab_im
    coef_re = (nr * lam_re + ni * lam_im) / den
    coef_im = (ni * lam_re - nr * lam_im) / den
    bb_re = coef_re[..., None] * b_re - coef_im[..., None] * b_im
    bb_im = coef_re[..., None] * b_im + coef_im[..., None] * b_re
    eye = jnp.eye(gpt, dtype=F32)

    def in_blk(bb):
        t = bb.reshape(nt, gpt, p, SSM_GROUP)
        return jnp.einsum('jgpc,gh->jgchp', t, eye).reshape(nt, gpt * SSM_GROUP, gpt * p)

    def out_blk(cc):
        t = cc.reshape(nt, gpt, SSM_GROUP, p)
        return jnp.einsum('jgcp,gh->jgphc', t, eye).reshape(nt, gpt * p, gpt * SSM_GROUP)

    bblk = jnp.concatenate([in_blk(bb_re), in_blk(bb_im)], axis=2).astype(BF16)
    cblk = jnp.concatenate([out_blk(c_re), -out_blk(c_im)], axis=1).astype(BF16)
    a_re = ab_re.reshape(nt, 1, gpt * p)
    a_im = ab_im.reshape(nt, 1, gpt * p)
    return bblk, cblk, a_re, a_im


def _tile_start(i, size):
    return i * size if isinstance(i, int) else pl.multiple_of(i * size, size)


def _attn_group(n_full):
    return max(g for g in range(2, ATT_GROUP_MAX + 1, 2) if n_full % g == 0)


def _attn_kernel(lam_ref, q_ref, k_ref, v_ref, g_ref, o_ref, vext_scr, sa_scr, sb_scr, m_scr, acc_scr,
                 *, tq, nq, group, lambda_init):
    vd = ATT_V_DIM
    h = tq // 2
    vext_scr[:, :vd] = v_ref[...]
    vext_scr[:, vd:] = jnp.ones((vext_scr.shape[0], vext_scr.shape[1] - vd), vext_scr.dtype)
    lane = lax.broadcasted_iota(jnp.int32, (h, q_ref.shape[1]), 1)
    bufs = (sa_scr, sb_scr)
    nt = (((1,), (1,)), ((), ()))

    def stacked_q(qi):
        r0 = _tile_start(qi, tq)
        parts = []
        for half in (q_ref[pl.ds(r0, h), :], q_ref[pl.ds(r0 + h, h), :]):
            zero = jnp.zeros_like(half)
            parts += [jnp.where(lane < ATT_HEAD_DIM, half, zero), jnp.where(lane >= ATT_HEAD_DIM, half, zero)]
        return jnp.concatenate(parts, axis=0)

    def keys(j, n):
        return k_ref[pl.ds(_tile_start(j, tq), n), :]

    def values(p, j, n):
        return jnp.dot(p.astype(BF16), vext_scr[pl.ds(_tile_start(j, tq), n), :],
                       preferred_element_type=F32)

    def full_scores(buf, qi, j):
        buf[...] = lax.dot_general(stacked_q(qi), keys(j, tq), nt, preferred_element_type=F32)

    def diagonal_scores(buf, qi):
        qs = stacked_q(qi)
        buf[:tq, :h] = lax.dot_general(qs[:tq], keys(qi, h), nt, preferred_element_type=F32)
        buf[tq:, :] = lax.dot_general(qs[tq:], keys(qi, tq), nt, preferred_element_type=F32)

    def first_block(buf, qi):
        row = lax.broadcasted_iota(jnp.int32, (tq, 1), 0) % h
        for rows, n, pos in ((slice(0, tq), h, row), (slice(tq, 2 * tq), tq, row + h)):
            s = buf[rows, :n]
            col = lax.broadcasted_iota(jnp.int32, s.shape, 1)
            s = jnp.where(col < (pos // CHUNK + 1) * CHUNK, s, NEG_BIG)
            m = jnp.broadcast_to(jnp.max(s, axis=1, keepdims=True), (tq, LANES))
            acc_scr[qi, rows, :] = values(jnp.exp2(s - jnp.tile(m, (1, n // LANES))), qi, n)
            m_scr[qi, rows, :] = m

    def next_block(buf, qi, j):
        s = buf[...]
        m_prev = m_scr[qi]
        m_new = jnp.maximum(m_prev, jnp.max(s, axis=1, keepdims=True))
        alpha = jnp.exp2(m_prev - m_new)
        pv = values(jnp.exp2(s - jnp.tile(m_new, (1, tq // LANES))), j, tq)
        acc_scr[qi] = jnp.tile(alpha, (1, acc_scr.shape[2] // LANES)) * acc_scr[qi] + pv
        m_scr[qi] = m_new

    def successor(qi, j):
        last = j + 1 == qi
        return (jnp.where(last, jnp.minimum(qi + 1, nq - 1), qi), jnp.where(last, 0, j + 1))

    def full_group(_, carry):
        qi, j = carry
        for g in range(group):
            nqi, nj = successor(qi, j)
            full_scores(bufs[(g + 1) % 2], nqi, nj)
            next_block(bufs[g % 2], qi, j)
            qi, j = nqi, nj
        return qi, j

    def normalise(qi, carry):
        acc = acc_scr[qi]
        o = acc[:, :vd] / acc[:, vd:2 * vd]
        lam = lam_ref[0]
        o = jnp.concatenate([o[:h] - lam * o[h:tq], o[tq:tq + h] - lam * o[tq + h:]], axis=0)
        o = o * lax.rsqrt(jnp.mean(o * o, axis=-1, keepdims=True) + RMS_EPS)
        o_ref[pl.ds(_tile_start(qi, tq), tq), :] = (
            (o * g_ref[...]) * (1.0 - lambda_init)).astype(o_ref.dtype)
        return carry

    diagonal_scores(bufs[0], 0)
    for qi in range(nq):
        if qi + 1 < nq:
            diagonal_scores(bufs[(qi + 1) % 2], qi + 1)
        else:
            full_scores(bufs[(qi + 1) % 2], 1, 0)
        first_block(bufs[qi % 2], qi)
    lax.fori_loop(0, nq * (nq - 1) // 2 // group, full_group, (jnp.int32(1), jnp.int32(0)))
    lax.fori_loop(0, nq, normalise, 0)


def _attention(lam, q, k, v, subln_g, tq, lambda_init):
    bsz, seq, w = q.shape
    hd = 2 * ATT_HEAD_DIM
    nq = seq // tq
    n_full = nq * (nq - 1) // 2
    assert nq % 2 == 0 and n_full % 2 == 0, "score buffers alternate, so both phases need even block counts"
    group = _attn_group(n_full)
    head = pl.BlockSpec((None, seq, hd), lambda b, h: (b, 0, h))
    return pl.pallas_call(
        functools.partial(_attn_kernel, tq=tq, nq=nq, group=group, lambda_init=lambda_init),
        out_shape=jax.ShapeDtypeStruct((bsz, seq, w), BF16),
        grid=(bsz, ATT_HEADS),
        in_specs=[pl.BlockSpec(memory_space=pltpu.SMEM), head, head, head,
                  pl.BlockSpec((1, ATT_V_DIM), lambda b, h: (0, 0))],
        out_specs=head,
        scratch_shapes=[pltpu.VMEM((seq, MXU_WIDTH), BF16),
                        pltpu.VMEM((2 * tq, tq), F32), pltpu.VMEM((2 * tq, tq), F32),
                        pltpu.VMEM((nq, 2 * tq, LANES), F32),
                        pltpu.VMEM((nq, 2 * tq, MXU_WIDTH), F32)],
        compiler_params=_params("parallel", "parallel"),
        name="attn",
    )(lam, q, k, v, subln_g)


def _ffn_chunks(dff):
    tiles = dff // MXU_WIDTH
    cut = (tiles + 1) // 2 * MXU_WIDTH
    return [(0, cut), (cut, dff)] if dff % MXU_WIDTH == 0 and tiles > 1 else [(0, dff)]


def _mix_kernel(x_ref, ys_ref, ya_ref, gs_ref, ga_ref, mod_ref, wba_ref, wout_ref, g1_ref, b1_ref,
                wup_ref, cw_ref, cb_ref, wdn_ref, g2_ref, b2_ref, o_ref, carry_scr, *, d, dff):
    @pl.when(pl.program_id(1) == 0)
    def _():
        carry_scr[...] = jnp.zeros_like(carry_scr)

    for rows in _row_groups(x_ref.shape[0], MIX_ROW_GROUP):
        n = rows.stop - rows.start
        att = jnp.dot(ya_ref[rows, :], wba_ref[...], preferred_element_type=F32)
        merged = (gs_ref[rows, :].astype(F32) * ys_ref[rows, :].astype(F32)
                  + ga_ref[rows, :].astype(F32) * att)
        out = jnp.dot(merged.astype(BF16), wout_ref[...], preferred_element_type=F32)
        x1 = (_layernorm(DEEPNORM_ALPHA * x_ref[rows, :] + mod_ref[:, 2 * d:3 * d] * out) * g1_ref[...]
              + b1_ref[...])

        h = (_layernorm(x1) * (1.0 + mod_ref[:, 4 * d:5 * d]) + mod_ref[:, 3 * d:4 * d]).astype(BF16)
        f = jnp.zeros((n, d), F32)
        for c0, c1 in _ffn_chunks(dff):
            a = jnp.dot(h, wup_ref[:, c0:c1], preferred_element_type=F32)
            val = jnp.dot(h, wup_ref[:, dff + c0:dff + c1], preferred_element_type=F32)
            row = lax.broadcasted_iota(jnp.int32, a.shape, 0)
            prev = carry_scr[:, c0:c1]
            p1 = jnp.broadcast_to(prev[7:8], a.shape)
            p2 = jnp.broadcast_to(prev[6:7], a.shape)
            a1 = jnp.where(row == 0, p1, pltpu.roll(a, 1, 0))
            a2 = jnp.where(row == 0, p2, jnp.where(row == 1, p1, pltpu.roll(a, 2, 0)))
            carry_scr[:, c0:c1] = a[n - 8:n]
            ac = cw_ref[0:1, c0:c1] * a2 + cw_ref[1:2, c0:c1] * a1 + cw_ref[2:3, c0:c1] * a + cb_ref[:, c0:c1]
            gl = (jax.nn.silu(ac) * val).astype(BF16)
            f = f + jnp.dot(gl, wdn_ref[c0:c1, :], preferred_element_type=F32)
        o_ref[rows, :] = (_layernorm(DEEPNORM_ALPHA * x1 + mod_ref[:, 5 * d:6 * d] * f) * g2_ref[...]
                          + b2_ref[...])


def _mix(x, ys2d, ya, gs, ga, mod3, w_ba, w_out, ln1_g, ln1_b, w_up, conv_w, conv_b, w_down, ln2_g, ln2_b, tm):
    bsz, seq, d = x.shape
    dff = w_down.shape[0]
    row = pl.BlockSpec((None, tm, d), lambda b, t: (b, t, 0))
    consts = (w_ba, w_out, ln1_g, ln1_b, w_up, conv_w, conv_b, w_down, ln2_g, ln2_b)
    return pl.pallas_call(
        functools.partial(_mix_kernel, d=d, dff=dff),
        out_shape=jax.ShapeDtypeStruct((bsz, seq, d), F32),
        grid=(bsz, seq // tm),
        in_specs=[row, pl.BlockSpec((tm, d), lambda b, t: (t, b)), row, row, row,
                  pl.BlockSpec((None, 1, mod3.shape[2]), lambda b, t: (b, 0, 0))]
                 + [_const_spec(w.shape) for w in consts],
        out_specs=row,
        scratch_shapes=[pltpu.VMEM((8, dff), F32)],
        compiler_params=_params("parallel", "arbitrary"),
        name="mix",
    )(x, ys2d, ya, gs, ga, mod3, *consts)


def kernel(x, c, w_mod, b_mod, w_in, ssm_lambda_re, ssm_lambda_im, ssm_log_dt, ssm_b_re, ssm_b_im, ssm_c_re, ssm_c_im, ssm_d, ssm_w_glu, ssm_b_glu, att_lambda_q1, att_lambda_k1, att_lambda_q2, att_lambda_k2, att_subln_g, w_branch_ssm, w_branch_att, w_out, ln1_g, ln1_b, w_up, conv_w, conv_b, w_down, ln2_g, ln2_b):
    bsz, seq, d = x.shape
    depth = w_mod.shape[0]
    assert depth == DEPTH
    tm = min(ROW_TILE, seq)
    ts = min(SCAN_STEPS, seq)
    tq = min(ATT_TILE, seq)
    for l in range(depth):
        mod3 = _mod(c, w_mod[l].astype(BF16), b_mod[l][None, :])[:, None, :]
        u2d, q, k, v, gs, ga = _inproj(x, mod3, w_in[l].astype(BF16), tm)

        bblk, cblk, a_re, a_im = _ssm_tables(ssm_lambda_re[l], ssm_lambda_im[l], ssm_log_dt[l],
                                             ssm_b_re[l], ssm_b_im[l], ssm_c_re[l], ssm_c_im[l])
        ys_t = _ssm(u2d.reshape(seq, bsz, d), bblk, cblk, a_re, a_im, ssm_d[l][None, :],
                    ssm_w_glu[l].astype(BF16), ssm_b_glu[l][None, :], w_branch_ssm[l].astype(BF16), ts)

        lam = (jnp.exp(jnp.sum(att_lambda_q1[l] * att_lambda_k1[l]))
               - jnp.exp(jnp.sum(att_lambda_q2[l] * att_lambda_k2[l])) + _lambda_init(l))
        ya = _attention(lam.reshape(1).astype(F32), q, k, v, att_subln_g[l][None, :], tq,
                        _lambda_init(l))

        x = _mix(x, ys_t.reshape(seq, bsz * d), ya, gs, ga, mod3, w_branch_att[l].astype(BF16),
                 w_out[l].astype(BF16), ln1_g[l][None, :], ln1_b[l][None, :],
                 w_up[l].astype(BF16), conv_w[l].reshape(CONV_W, -1), conv_b[l][None, :],
                 w_down[l].astype(BF16), ln2_g[l][None, :], ln2_b[l][None, :], tm)
    return x
```

```python
import functools
import math

import jax
import jax.numpy as jnp
from jax import lax
from jax.experimental import pallas as pl
from jax.experimental.pallas import tpu as pltpu

F32 = jnp.float32
BF16 = jnp.bfloat16

CHUNK = 64
ATT_HEADS = 8
ATT_HEAD_DIM = 64
ATT_V_DIM = 2 * ATT_HEAD_DIM
SSM_GROUP = 16
SSM_STATE = 64
CONV_W = 3
DEPTH = 1
DEEPNORM_ALPHA = (2.0 * DEPTH) ** 0.25
LN_EPS = 1e-5
RMS_EPS = 1e-5

def _lambda_init(layer):
    return 0.8 - 0.6 * math.exp(-0.3 * layer)


LANES = 128
MXU_WIDTH = 256
VMEM_LIMIT_BYTES = 56 * 1024 * 1024

GROUP_TILE = 256
SCAN_LANES = 256
FFN_CHUNK_TILES = 6
INPROJ_ROW_GROUP = 256
MIX_ROW_GROUP = 512
ROW_TILE = 512
SCAN_STEPS = 32
ATT_TILE = 512
ATT_GROUP_MAX = 14
NEG_BIG = -1e30


def _layernorm(x):
    mu = jnp.mean(x, axis=-1, keepdims=True)
    xc = x - mu
    var = jnp.mean(xc * xc, axis=-1, keepdims=True)
    return xc * lax.rsqrt(var + LN_EPS)


def _row_groups(tm, group):
    g = min(group, tm)
    return [slice(r, r + g) for r in range(0, tm, g)]


def _const_spec(shape):
    nd = len(shape)
    return pl.BlockSpec(shape, lambda *_: (0,) * nd, pipeline_mode=pl.Buffered(1))


def _params(*sem):
    return pltpu.CompilerParams(dimension_semantics=sem, vmem_limit_bytes=VMEM_LIMIT_BYTES)


def _mod_kernel(c_ref, w_ref, b_ref, o_ref):
    cond = jax.nn.silu(c_ref[...]).astype(BF16)
    o_ref[...] = jnp.dot(cond, w_ref[...], preferred_element_type=F32) + b_ref[...]


def _mod(c, w_mod, b_mod):
    bsz, d = c.shape
    n = w_mod.shape[1]
    tn = d
    return pl.pallas_call(
        _mod_kernel,
        out_shape=jax.ShapeDtypeStruct((bsz, n), F32),
        grid=(n // tn,),
        in_specs=[pl.BlockSpec((bsz, d), lambda j: (0, 0)),
                  pl.BlockSpec((d, tn), lambda j: (0, j)),
                  pl.BlockSpec((1, tn), lambda j: (0, j))],
        out_specs=pl.BlockSpec((bsz, tn), lambda j: (0, j)),
        compiler_params=_params("arbitrary"),
        name="mod",
    )(c, w_mod, b_mod)


def _inproj_kernel(x_ref, mod_ref, w_ref, u_ref, q_ref, k_ref, v_ref, gs_ref, ga_ref, *, d, q_scale):
    for rows in _row_groups(x_ref.shape[0], INPROJ_ROW_GROUP):
        h = _layernorm(x_ref[rows, :])
        h = (h * (1.0 + mod_ref[:, d:2 * d]) + mod_ref[:, 0:d]).astype(BF16)

        def proj(n, h=h):
            return jnp.dot(h, w_ref[:, n * d:(n + 1) * d], preferred_element_type=F32)

        u_ref[rows, :] = proj(0).astype(BF16)
        q_ref[rows, :] = (proj(1) * q_scale).astype(BF16)
        k_ref[rows, :] = proj(2).astype(BF16)
        v_ref[rows, :] = proj(3).astype(BF16)
        gs_ref[rows, :] = jax.nn.sigmoid(proj(4)).astype(BF16)
        ga_ref[rows, :] = jax.nn.sigmoid(proj(5)).astype(BF16)


def _inproj(x, mod3, w_in, tm):
    bsz, seq, d = x.shape
    row = pl.BlockSpec((None, tm, d), lambda b, t: (b, t, 0))
    out_row = jax.ShapeDtypeStruct((bsz, seq, d), BF16)
    return pl.pallas_call(
        functools.partial(_inproj_kernel, d=d, q_scale=ATT_HEAD_DIM ** -0.5 * math.log2(math.e)),
        out_shape=(jax.ShapeDtypeStruct((seq, bsz * d), BF16),) + (out_row,) * 5,
        grid=(bsz, seq // tm),
        in_specs=[row,
                  pl.BlockSpec((None, 1, mod3.shape[2]), lambda b, t: (b, 0, 0)),
                  _const_spec(w_in.shape)],
        out_specs=(pl.BlockSpec((tm, d), lambda b, t: (t, b)),) + (row,) * 5,
        compiler_params=_params("parallel", "parallel"),
        name="inproj",
    )(x, mod3, w_in)


def _ssm_kernel(u_ref, bblk_ref, cblk_ref, are_ref, aim_ref, d_ref, wglu_ref, bglu_ref, wbr_ref,
                out_ref, state_scr, bu_scr, sbf_scr, y_scr, *, ts, bsz, d):
    rows = ts * bsz
    n_tiles = d // GROUP_TILE
    st = GROUP_TILE * SSM_STATE // SSM_GROUP

    @pl.when(pl.program_id(0) == 0)
    def _():
        state_scr[...] = jnp.zeros_like(state_scr)

    u = u_ref[...].reshape(rows, d)
    for j in range(n_tiles):
        cols = slice(j * GROUP_TILE, (j + 1) * GROUP_TILE)
        buf = j % 2
        bu_scr[buf] = jnp.dot(u[:, cols], bblk_ref[j], preferred_element_type=F32)
        for c in range(st // SCAN_LANES):
            re = slice(c * SCAN_LANES, (c + 1) * SCAN_LANES)
            im = slice(st + c * SCAN_LANES, st + (c + 1) * SCAN_LANES)
            ar = jnp.broadcast_to(are_ref[j, :, re], (bsz, SCAN_LANES))
            ai = jnp.broadcast_to(aim_ref[j, :, re], (bsz, SCAN_LANES))
            sr = state_scr[j, :, re]
            si = state_scr[j, :, im]
            for t in range(ts):
                r = slice(t * bsz, (t + 1) * bsz)
                sr, si = (ar * sr - ai * si + bu_scr[buf, r, re],
                          ar * si + ai * sr + bu_scr[buf, r, im])
                sbf_scr[buf, r, re] = sr.astype(BF16)
                sbf_scr[buf, r, im] = si.astype(BF16)
            state_scr[j, :, re] = sr
            state_scr[j, :, im] = si
        y_scr[:, cols] = (jnp.dot(sbf_scr[buf], cblk_ref[j], preferred_element_type=F32)
                          + d_ref[:, cols] * u[:, cols].astype(F32))

    z = jax.nn.gelu(y_scr[...])
    gate = jax.nn.sigmoid(jnp.dot(z.astype(BF16), wglu_ref[...], preferred_element_type=F32) + bglu_ref[...])
    out = jnp.dot((z * gate).astype(BF16), wbr_ref[...], preferred_element_type=F32)
    out_ref[...] = out.astype(BF16).reshape(ts, bsz, d)


def _ssm(u_t, bblk, cblk, a_re, a_im, d_skip, w_glu, b_glu, w_br, ts):
    seq, bsz, d = u_t.shape
    rows = ts * bsz
    st2 = bblk.shape[2]
    blk = pl.BlockSpec((ts, bsz, d), lambda i: (i, 0, 0))
    return pl.pallas_call(
        functools.partial(_ssm_kernel, ts=ts, bsz=bsz, d=d),
        out_shape=jax.ShapeDtypeStruct((seq, bsz, d), BF16),
        grid=(seq // ts,),
        in_specs=[blk, _const_spec(bblk.shape), _const_spec(cblk.shape), _const_spec(a_re.shape),
                  _const_spec(a_im.shape), _const_spec(d_skip.shape), _const_spec(w_glu.shape),
                  _const_spec(b_glu.shape), _const_spec(w_br.shape)],
        out_specs=blk,
        scratch_shapes=[pltpu.VMEM((bblk.shape[0], bsz, st2), F32),
                        pltpu.VMEM((2, rows, st2), F32),
                        pltpu.VMEM((2, rows, st2), BF16),
                        pltpu.VMEM((rows, d), F32)],
        compiler_params=_params("arbitrary"),
        name="ssm",
    )(u_t, bblk, cblk, a_re, a_im, d_skip, w_glu, b_glu, w_br)


def _ssm_tables(lam_re, lam_im, log_dt, b_re, b_im, c_re, c_im):
    g, p = lam_re.shape
    gpt = GROUP_TILE // SSM_GROUP
    nt = g // gpt
    dt = jnp.exp(log_dt)[:, None]
    mag = jnp.exp(lam_re * dt)
    ab_re = mag * jnp.cos(lam_im * dt)
    ab_im = mag * jnp.sin(lam_im * dt)
    den = lam_re * lam_re + lam_im * lam_im
    nr = ab_re - 1.0
    ni = ab_im
    coef_re = (nr * lam_re + ni * lam_im) / den
    coef_im = (ni * lam_re - nr * lam_im) / den
    bb_re = coef_re[..., None] * b_re - coef_im[..., None] * b_im
    bb_im = coef_re[..., None] * b_im + coef_im[..., None] * b_re
    eye = jnp.eye(gpt, dtype=F32)

    def in_blk(bb):
        t = bb.reshape(nt, gpt, p, SSM_GROUP)
        return jnp.einsum('jgpc,gh->jgchp', t, eye).reshape(nt, gpt * SSM_GROUP, gpt * p)

    def out_blk(cc):
        t = cc.reshape(nt, gpt, SSM_GROUP, p)
        return jnp.einsum('jgcp,gh->jgphc', t, eye).reshape(nt, gpt * p, gpt * SSM_GROUP)

    bblk = jnp.concatenate([in_blk(bb_re), in_blk(bb_im)], axis=2).astype(BF16)
    cblk = jnp.concatenate([out_blk(c_re), -out_blk(c_im)], axis=1).astype(BF16)
    a_re = ab_re.reshape(nt, 1, gpt * p)
    a_im = ab_im.reshape(nt, 1, gpt * p)
    return bblk, cblk, a_re, a_im


def _tile_start(i, size):
    return i * size if isinstance(i, int) else pl.multiple_of(i * size, size)


def _attn_group(n_full):
    return max(g for g in range(2, ATT_GROUP_MAX + 1, 2) if n_full % g == 0)


def _attn_kernel(lam_ref, q_ref, k_ref, v_ref, g_ref, o_ref, vext_scr, sa_scr, sb_scr, m_scr, acc_scr,
                 *, tq, nq, group, lambda_init):
    vd = ATT_V_DIM
    h = tq // 2
    vext_scr[:, :vd] = v_ref[...]
    vext_scr[:, vd:] = jnp.ones((vext_scr.shape[0], vext_scr.shape[1] - vd), vext_scr.dtype)
    lane = lax.broadcasted_iota(jnp.int32, (h, q_ref.shape[1]), 1)
    bufs = (sa_scr, sb_scr)
    nt = (((1,), (1,)), ((), ()))

    def stacked_q(qi):
        r0 = _tile_start(qi, tq)
        parts = []
        for half in (q_ref[pl.ds(r0, h), :], q_ref[pl.ds(r0 + h, h), :]):
            zero = jnp.zeros_like(half)
            parts += [jnp.where(lane < ATT_HEAD_DIM, half, zero), jnp.where(lane >= ATT_HEAD_DIM, half, zero)]
        return jnp.concatenate(parts, axis=0)

    def keys(j, n):
        return k_ref[pl.ds(_tile_start(j, tq), n), :]

    def weights(s, m):
        return jnp.exp2(s - jnp.tile(m, (1, s.shape[1] // LANES))).astype(BF16)

    def values(p, j, n):
        return jnp.dot(p, vext_scr[pl.ds(_tile_start(j, tq), n), :],
                       preferred_element_type=F32)

    def full_scores(buf, qi, j):
        buf[...] = lax.dot_general(stacked_q(qi), keys(j, tq), nt, preferred_element_type=F32)

    def diagonal_scores(buf, qi):
        qs = stacked_q(qi)
        buf[:tq, :h] = lax.dot_general(qs[:tq], keys(qi, h), nt, preferred_element_type=F32)
        buf[tq:, :] = lax.dot_general(qs[tq:], keys(qi, tq), nt, preferred_element_type=F32)

    def first_block(buf, qi):
        row = lax.broadcasted_iota(jnp.int32, (tq, 1), 0) % h
        for rows, n, pos in ((slice(0, tq), h, row), (slice(tq, 2 * tq), tq, row + h)):
            s = buf[rows, :n]
            col = lax.broadcasted_iota(jnp.int32, s.shape, 1)
            s = jnp.where(col < (pos // CHUNK + 1) * CHUNK, s, NEG_BIG)
            m = jnp.broadcast_to(jnp.max(s, axis=1, keepdims=True), (tq, LANES))
            acc_scr[qi, rows, :] = values(weights(s, m), qi, n)
            m_scr[qi, rows, :] = m

    def next_block(buf, qi, j):
        s = buf[...]
        m_prev = m_scr[qi]
        m_new = jnp.maximum(m_prev, jnp.max(s, axis=1, keepdims=True))
        alpha = jnp.exp2(m_prev - m_new)
        pv = values(weights(s, m_new), j, tq)
        acc_scr[qi] = jnp.tile(alpha, (1, acc_scr.shape[2] // LANES)) * acc_scr[qi] + pv
        m_scr[qi] = m_new

    def normalise(qi):
        acc = acc_scr[qi]
        o = acc[:, :vd] / acc[:, vd:2 * vd]
        lam = lam_ref[0]
        o = jnp.concatenate([o[:h] - lam * o[h:tq], o[tq:tq + h] - lam * o[tq + h:]], axis=0)
        o = o * lax.rsqrt(jnp.mean(o * o, axis=-1, keepdims=True) + RMS_EPS)
        o_ref[pl.ds(_tile_start(qi, tq), tq), :] = (
            (o * g_ref[...]) * (1.0 - lambda_init)).astype(o_ref.dtype)

    def successor(qi, j):
        last = j + 1 == qi
        return (jnp.where(last, jnp.minimum(qi + 1, nq - 1), qi), jnp.where(last, 0, j + 1))

    def full_group(_, carry):
        qi, j = carry
        for g in range(group):
            nqi, nj = successor(qi, j)
            full_scores(bufs[(g + 1) % 2], nqi, nj)
            next_block(bufs[g % 2], qi, j)
            qi, j = nqi, nj
        return qi, j

    diagonal_scores(bufs[0], 0)
    for qi in range(nq):
        if qi + 1 < nq:
            diagonal_scores(bufs[(qi + 1) % 2], qi + 1)
        else:
            full_scores(bufs[(qi + 1) % 2], 1, 0)
        first_block(bufs[qi % 2], qi)
    lax.fori_loop(0, nq * (nq - 1) // 2 // group, full_group, (jnp.int32(1), jnp.int32(0)))
    for qi in range(nq):
        normalise(qi)


def _attention(lam, q, k, v, subln_g, tq, lambda_init):
    bsz, seq, w = q.shape
    hd = 2 * ATT_HEAD_DIM
    nq = seq // tq
    n_full = nq * (nq - 1) // 2
    assert nq % 2 == 0 and n_full % 2 == 0, "score buffers alternate, so both phases need even block counts"
    group = _attn_group(n_full)
    head = pl.BlockSpec((None, seq, hd), lambda b, h: (b, 0, h))
    return pl.pallas_call(
        functools.partial(_attn_kernel, tq=tq, nq=nq, group=group, lambda_init=lambda_init),
        out_shape=jax.ShapeDtypeStruct((bsz, seq, w), BF16),
        grid=(bsz, ATT_HEADS),
        in_specs=[pl.BlockSpec(memory_space=pltpu.SMEM), head, head, head,
                  pl.BlockSpec((1, ATT_V_DIM), lambda b, h: (0, 0))],
        out_specs=head,
        scratch_shapes=[pltpu.VMEM((seq, MXU_WIDTH), BF16),
                        pltpu.VMEM((2 * tq, tq), F32), pltpu.VMEM((2 * tq, tq), F32),
                        pltpu.VMEM((nq, 2 * tq, LANES), F32),
                        pltpu.VMEM((nq, 2 * tq, MXU_WIDTH), F32)],
        compiler_params=_params("parallel", "parallel"),
        name="attn",
    )(lam, q, k, v, subln_g)


def _ffn_chunks(dff):
    step = FFN_CHUNK_TILES * MXU_WIDTH
    return [(c, min(c + step, dff)) for c in range(0, dff, step)]


def _mix_kernel(x_ref, ys_ref, ya_ref, gs_ref, ga_ref, mod_ref, wba_ref, wout_ref, g1_ref, b1_ref,
                wup_ref, cw_ref, cb_ref, wdn_ref, g2_ref, b2_ref, o_ref, carry_scr, *, d, dff):
    @pl.when(pl.program_id(1) == 0)
    def _():
        carry_scr[...] = jnp.zeros_like(carry_scr)

    for rows in _row_groups(x_ref.shape[0], MIX_ROW_GROUP):
        n = rows.stop - rows.start
        att = jnp.dot(ya_ref[rows, :], wba_ref[...], preferred_element_type=F32)
        merged = (gs_ref[rows, :].astype(F32) * ys_ref[rows, :].astype(F32)
                  + ga_ref[rows, :].astype(F32) * att)
        out = jnp.dot(merged.astype(BF16), wout_ref[...], preferred_element_type=F32)
        x1 = (_layernorm(DEEPNORM_ALPHA * x_ref[rows, :] + mod_ref[:, 2 * d:3 * d] * out) * g1_ref[...]
              + b1_ref[...])

        h = (_layernorm(x1) * (1.0 + mod_ref[:, 4 * d:5 * d]) + mod_ref[:, 3 * d:4 * d]).astype(BF16)
        f = jnp.zeros((n, d), F32)
        for c0, c1 in _ffn_chunks(dff):
            a = jnp.dot(h, wup_ref[:, c0:c1], preferred_element_type=F32)
            val = jnp.dot(h, wup_ref[:, dff + c0:dff + c1], preferred_element_type=F32)
            row = lax.broadcasted_iota(jnp.int32, a.shape, 0)
            prev = carry_scr[:, c0:c1]
            p1 = jnp.broadcast_to(prev[7:8], a.shape)
            p2 = jnp.broadcast_to(prev[6:7], a.shape)
            a1 = jnp.where(row == 0, p1, pltpu.roll(a, 1, 0))
            a2 = jnp.where(row == 0, p2, jnp.where(row == 1, p1, pltpu.roll(a, 2, 0)))
            carry_scr[:, c0:c1] = a[n - 8:n]
            ac = cw_ref[0:1, c0:c1] * a2 + cw_ref[1:2, c0:c1] * a1 + cw_ref[2:3, c0:c1] * a + cb_ref[:, c0:c1]
            gl = (jax.nn.silu(ac) * val).astype(BF16)
            f = f + jnp.dot(gl, wdn_ref[c0:c1, :], preferred_element_type=F32)
        o_ref[rows, :] = (_layernorm(DEEPNORM_ALPHA * x1 + mod_ref[:, 5 * d:6 * d] * f) * g2_ref[...]
                          + b2_ref[...])


def _mix(x, ys2d, ya, gs, ga, mod3, w_ba, w_out, ln1_g, ln1_b, w_up, conv_w, conv_b, w_down, ln2_g, ln2_b, tm):
    bsz, seq, d = x.shape
    dff = w_down.shape[0]
    row = pl.BlockSpec((None, tm, d), lambda b, t: (b, t, 0))
    consts = (w_ba, w_out, ln1_g, ln1_b, w_up, conv_w, conv_b, w_down, ln2_g, ln2_b)
    return pl.pallas_call(
        functools.partial(_mix_kernel, d=d, dff=dff),
        out_shape=jax.ShapeDtypeStruct((bsz, seq, d), F32),
        grid=(bsz, seq // tm),
        in_specs=[row, pl.BlockSpec((tm, d), lambda b, t: (t, b)), row, row, row,
                  pl.BlockSpec((None, 1, mod3.shape[2]), lambda b, t: (b, 0, 0))]
                 + [_const_spec(w.shape) for w in consts],
        out_specs=row,
        scratch_shapes=[pltpu.VMEM((8, dff), F32)],
        compiler_params=_params("parallel", "arbitrary"),
        name="mix",
    )(x, ys2d, ya, gs, ga, mod3, *consts)


def kernel(x, c, w_mod, b_mod, w_in, ssm_lambda_re, ssm_lambda_im, ssm_log_dt, ssm_b_re, ssm_b_im, ssm_c_re, ssm_c_im, ssm_d, ssm_w_glu, ssm_b_glu, att_lambda_q1, att_lambda_k1, att_lambda_q2, att_lambda_k2, att_subln_g, w_branch_ssm, w_branch_att, w_out, ln1_g, ln1_b, w_up, conv_w, conv_b, w_down, ln2_g, ln2_b):
    bsz, seq, d = x.shape
    depth = w_mod.shape[0]
    assert depth == DEPTH
    tm = min(ROW_TILE, seq)
    ts = min(SCAN_STEPS, seq)
    tq = min(ATT_TILE, seq)
    for l in range(depth):
        mod3 = _mod(c, w_mod[l].astype(BF16), b_mod[l][None, :])[:, None, :]
        u2d, q, k, v, gs, ga = _inproj(x, mod3, w_in[l].astype(BF16), tm)

        bblk, cblk, a_re, a_im = _ssm_tables(ssm_lambda_re[l], ssm_lambda_im[l], ssm_log_dt[l],
                                             ssm_b_re[l], ssm_b_im[l], ssm_c_re[l], ssm_c_im[l])
        ys_t = _ssm(u2d.reshape(seq, bsz, d), bblk, cblk, a_re, a_im, ssm_d[l][None, :],
                    ssm_w_glu[l].astype(BF16), ssm_b_glu[l][None, :], w_branch_ssm[l].astype(BF16), ts)

        lam = (jnp.exp(jnp.sum(att_lambda_q1[l] * att_lambda_k1[l]))
               - jnp.exp(jnp.sum(att_lambda_q2[l] * att_lambda_k2[l])) + _lambda_init(l))
        ya = _attention(lam.reshape(1).astype(F32), q, k, v, att_subln_g[l][None, :], tq,
                        _lambda_init(l))

        x = _mix(x, ys_t.reshape(seq, bsz * d), ya, gs, ga, mod3, w_branch_att[l].astype(BF16),
                 w_out[l].astype(BF16), ln1_g[l][None, :], ln1_b[l][None, :],
                 w_up[l].astype(BF16), conv_w[l].reshape(CONV_W, -1), conv_b[l][None, :],
                 w_down[l].astype(BF16), ln2_g[l][None, :], ln2_b[l][None, :], tm)
    return x
```

```python
import functools
import math

import jax
import jax.numpy as jnp
from jax import lax
from jax.experimental import pallas as pl
from jax.experimental.pallas import tpu as pltpu

F32 = jnp.float32
BF16 = jnp.bfloat16

CHUNK = 64
ATT_HEADS = 8
ATT_HEAD_DIM = 64
ATT_V_DIM = 2 * ATT_HEAD_DIM
SSM_GROUP = 16
SSM_STATE = 64
CONV_W = 3
DEPTH = 1
DEEPNORM_ALPHA = (2.0 * DEPTH) ** 0.25
LN_EPS = 1e-5
RMS_EPS = 1e-5

def _lambda_init(layer):
    return 0.8 - 0.6 * math.exp(-0.3 * layer)


LANES = 128
MXU_WIDTH = 256
VMEM_LIMIT_BYTES = 56 * 1024 * 1024

GROUP_TILE = 256
SCAN_LANES = 256
FFN_CHUNK_TILES = 6
INPROJ_ROW_GROUP = 256
MIX_ROW_GROUP = 512
ROW_TILE = 512
SCAN_STEPS = 32
ATT_TILE = 512
ATT_GROUP_MAX = 14
NEG_BIG = -1e30


def _layernorm(x):
    mu = jnp.mean(x, axis=-1, keepdims=True)
    xc = x - mu
    var = jnp.mean(xc * xc, axis=-1, keepdims=True)
    return xc * lax.rsqrt(var + LN_EPS)


def _row_groups(tm, group):
    g = min(group, tm)
    return [slice(r, r + g) for r in range(0, tm, g)]


def _const_spec(shape):
    nd = len(shape)
    return pl.BlockSpec(shape, lambda *_: (0,) * nd, pipeline_mode=pl.Buffered(1))


def _params(*sem):
    return pltpu.CompilerParams(dimension_semantics=sem, vmem_limit_bytes=VMEM_LIMIT_BYTES)


def _mod_kernel(c_ref, w_ref, b_ref, o_ref):
    cond = jax.nn.silu(c_ref[...]).astype(BF16)
    o_ref[...] = jnp.dot(cond, w_ref[...], preferred_element_type=F32) + b_ref[...]


def _mod(c, w_mod, b_mod):
    bsz, d = c.shape
    n = w_mod.shape[1]
    tn = d
    return pl.pallas_call(
        _mod_kernel,
        out_shape=jax.ShapeDtypeStruct((bsz, n), F32),
        grid=(n // tn,),
        in_specs=[pl.BlockSpec((bsz, d), lambda j: (0, 0)),
                  pl.BlockSpec((d, tn), lambda j: (0, j)),
                  pl.BlockSpec((1, tn), lambda j: (0, j))],
        out_specs=pl.BlockSpec((bsz, tn), lambda j: (0, j)),
        compiler_params=_params("arbitrary"),
        name="mod",
    )(c, w_mod, b_mod)


def _inproj_kernel(x_ref, mod_ref, w_ref, u_ref, q_ref, k_ref, v_ref, gs_ref, ga_ref, *, d, q_scale):
    for rows in _row_groups(x_ref.shape[0], INPROJ_ROW_GROUP):
        h = _layernorm(x_ref[rows, :])
        h = (h * (1.0 + mod_ref[:, d:2 * d]) + mod_ref[:, 0:d]).astype(BF16)

        def proj(n, h=h):
            return jnp.dot(h, w_ref[:, n * d:(n + 1) * d], preferred_element_type=F32)

        u_ref[rows, :] = proj(0).astype(BF16)
        q_ref[rows, :] = (proj(1) * q_scale).astype(BF16)
        k_ref[rows, :] = proj(2).astype(BF16)
        v_ref[rows, :] = proj(3).astype(BF16)
        gs_ref[rows, :] = jax.nn.sigmoid(proj(4)).astype(BF16)
        ga_ref[rows, :] = jax.nn.sigmoid(proj(5)).astype(BF16)


def _inproj(x, mod3, w_in, tm):
    bsz, seq, d = x.shape
    row = pl.BlockSpec((None, tm, d), lambda b, t: (b, t, 0))
    out_row = jax.ShapeDtypeStruct((bsz, seq, d), BF16)
    return pl.pallas_call(
        functools.partial(_inproj_kernel, d=d, q_scale=ATT_HEAD_DIM ** -0.5 * math.log2(math.e)),
        out_shape=(jax.ShapeDtypeStruct((seq, bsz * d), BF16),) + (out_row,) * 5,
        grid=(bsz, seq // tm),
        in_specs=[row,
                  pl.BlockSpec((None, 1, mod3.shape[2]), lambda b, t: (b, 0, 0)),
                  _const_spec(w_in.shape)],
        out_specs=(pl.BlockSpec((tm, d), lambda b, t: (t, b)),) + (row,) * 5,
        compiler_params=_params("parallel", "parallel"),
        name="inproj",
    )(x, mod3, w_in)


def _ssm_kernel(u_ref, bblk_ref, cblk_ref, are_ref, aim_ref, d_ref, wglu_ref, bglu_ref, wbr_ref,
                out_ref, state_scr, bu_scr, sbf_scr, y_scr, *, ts, bsz, d):
    rows = ts * bsz
    n_tiles = d // GROUP_TILE
    st = GROUP_TILE * SSM_STATE // SSM_GROUP

    @pl.when(pl.program_id(0) == 0)
    def _():
        state_scr[...] = jnp.zeros_like(state_scr)

    u = u_ref[...].reshape(rows, d)
    for j in range(n_tiles):
        cols = slice(j * GROUP_TILE, (j + 1) * GROUP_TILE)
        buf = j % 2
        bu_scr[buf] = jnp.dot(u[:, cols], bblk_ref[j], preferred_element_type=F32)
        for c in range(st // SCAN_LANES):
            re = slice(c * SCAN_LANES, (c + 1) * SCAN_LANES)
            im = slice(st + c * SCAN_LANES, st + (c + 1) * SCAN_LANES)
            ar = jnp.broadcast_to(are_ref[j, :, re], (bsz, SCAN_LANES))
            ai = jnp.broadcast_to(aim_ref[j, :, re], (bsz, SCAN_LANES))
            sr = state_scr[j, :, re]
            si = state_scr[j, :, im]
            for t in range(ts):
                r = slice(t * bsz, (t + 1) * bsz)
                sr, si = (ar * sr - ai * si + bu_scr[buf, r, re],
                          ar * si + ai * sr + bu_scr[buf, r, im])
                sbf_scr[buf, r, re] = sr.astype(BF16)
                sbf_scr[buf, r, im] = si.astype(BF16)
            state_scr[j, :, re] = sr
            state_scr[j, :, im] = si
        y_scr[:, cols] = (jnp.dot(sbf_scr[buf], cblk_ref[j], preferred_element_type=F32)
                          + d_ref[:, cols] * u[:, cols].astype(F32))

    z = jax.nn.gelu(y_scr[...])
    gate = jax.nn.sigmoid(jnp.dot(z.astype(BF16), wglu_ref[...], preferred_element_type=F32) + bglu_ref[...])
    out = jnp.dot((z * gate).astype(BF16), wbr_ref[...], preferred_element_type=F32)
    out_ref[...] = out.astype(BF16).reshape(ts, bsz, d)


def _ssm(u_t, bblk, cblk, a_re, a_im, d_skip, w_glu, b_glu, w_br, ts):
    seq, bsz, d = u_t.shape
    rows = ts * bsz
    st2 = bblk.shape[2]
    blk = pl.BlockSpec((ts, bsz, d), lambda i: (i, 0, 0))
    return pl.pallas_call(
        functools.partial(_ssm_kernel, ts=ts, bsz=bsz, d=d),
        out_shape=jax.ShapeDtypeStruct((seq, bsz, d), BF16),
        grid=(seq // ts,),
        in_specs=[blk, _const_spec(bblk.shape), _const_spec(cblk.shape), _const_spec(a_re.shape),
                  _const_spec(a_im.shape), _const_spec(d_skip.shape), _const_spec(w_glu.shape),
                  _const_spec(b_glu.shape), _const_spec(w_br.shape)],
        out_specs=blk,
        scratch_shapes=[pltpu.VMEM((bblk.shape[0], bsz, st2), F32),
                        pltpu.VMEM((2, rows, st2), F32),
                        pltpu.VMEM((2, rows, st2), BF16),
                        pltpu.VMEM((rows, d), F32)],
        compiler_params=_params("arbitrary"),
        name="ssm",
    )(u_t, bblk, cblk, a_re, a_im, d_skip, w_glu, b_glu, w_br)


def _ssm_tables(lam_re, lam_im, log_dt, b_re, b_im, c_re, c_im):
    g, p = lam_re.shape
    gpt = GROUP_TILE // SSM_GROUP
    nt = g // gpt
    dt = jnp.exp(log_dt)[:, None]
    mag = jnp.exp(lam_re * dt)
    ab_re = mag * jnp.cos(lam_im * dt)
    ab_im = mag * jnp.sin(lam_im * dt)
    den = lam_re * lam_re + lam_im * lam_im
    nr = ab_re - 1.0
    ni = ab_im
    coef_re = (nr * lam_re + ni * lam_im) / den
    coef_im = (ni * lam_re - nr * lam_im) / den
    bb_re = coef_re[..., None] * b_re - coef_im[..., None] * b_im
    bb_im = coef_re[..., None] * b_im + coef_im[..., None] * b_re
    same_group = jnp.eye(gpt, dtype=bool)
    w = jnp.stack([bb_re, bb_im]).reshape(2, nt, gpt, p, SSM_GROUP).transpose(1, 2, 4, 0, 3)
    bblk = jnp.where(same_group[None, :, None, None, :, None], w[:, :, :, :, None, :], 0.0)
    bblk = bblk.astype(BF16).reshape(nt, gpt * SSM_GROUP, 2 * gpt * p)
    w = jnp.stack([c_re, -c_im]).reshape(2, nt, gpt, SSM_GROUP, p).transpose(1, 0, 2, 4, 3)
    cblk = jnp.where(same_group[None, None, :, None, :, None], w[:, :, :, :, None, :], 0.0)
    cblk = cblk.astype(BF16).reshape(nt, 2 * gpt * p, gpt * SSM_GROUP)
    a_re = ab_re.reshape(nt, 1, gpt * p)
    a_im = ab_im.reshape(nt, 1, gpt * p)
    return bblk, cblk, a_re, a_im


def _tile_start(i, size):
    return i * size if isinstance(i, int) else pl.multiple_of(i * size, size)


def _attn_group(n_full):
    return max(g for g in range(2, ATT_GROUP_MAX + 1, 2) if n_full % g == 0)


def _attn_kernel(lam_ref, q_ref, k_ref, v_ref, g_ref, o_ref, vext_scr, sa_scr, sb_scr, m_scr, acc_scr,
                 *, tq, nq, group, lambda_init):
    vd = ATT_V_DIM
    h = tq // 2
    vext_scr[:, :vd] = v_ref[...]
    vext_scr[:, vd:] = jnp.ones((vext_scr.shape[0], vext_scr.shape[1] - vd), vext_scr.dtype)
    lane = lax.broadcasted_iota(jnp.int32, (h, q_ref.shape[1]), 1)
    bufs = (sa_scr, sb_scr)
    nt = (((1,), (1,)), ((), ()))

    def stacked_q(qi):
        r0 = _tile_start(qi, tq)
        parts = []
        for half in (q_ref[pl.ds(r0, h), :], q_ref[pl.ds(r0 + h, h), :]):
            zero = jnp.zeros_like(half)
            parts += [jnp.where(lane < ATT_HEAD_DIM, half, zero), jnp.where(lane >= ATT_HEAD_DIM, half, zero)]
        return jnp.concatenate(parts, axis=0)

    def keys(j, n):
        return k_ref[pl.ds(_tile_start(j, tq), n), :]

    def weights(s, m):
        return jnp.exp2(s - jnp.tile(m, (1, s.shape[1] // LANES))).astype(BF16)

    def values(p, j, n):
        return jnp.dot(p, vext_scr[pl.ds(_tile_start(j, tq), n), :],
                       preferred_element_type=F32)

    def full_scores(buf, qi, j):
        buf[...] = lax.dot_general(stacked_q(qi), keys(j, tq), nt, preferred_element_type=F32)

    def diagonal_scores(buf, qi):
        qs = stacked_q(qi)
        buf[:tq, :h] = lax.dot_general(qs[:tq], keys(qi, h), nt, preferred_element_type=F32)
        buf[tq:, :] = lax.dot_general(qs[tq:], keys(qi, tq), nt, preferred_element_type=F32)

    def first_block(buf, qi):
        row = lax.broadcasted_iota(jnp.int32, (tq, 1), 0) % h
        for rows, n, pos in ((slice(0, tq), h, row), (slice(tq, 2 * tq), tq, row + h)):
            s = buf[rows, :n]
            col = lax.broadcasted_iota(jnp.int32, s.shape, 1)
            s = jnp.where(col < (pos // CHUNK + 1) * CHUNK, s, NEG_BIG)
            m = jnp.broadcast_to(jnp.max(s, axis=1, keepdims=True), (tq, LANES))
            acc_scr[qi, rows, :] = values(weights(s, m), qi, n)
            m_scr[qi, rows, :] = m

    def next_block(buf, qi, j):
        s = buf[...]
        m_prev = m_scr[qi]
        m_new = jnp.maximum(m_prev, jnp.max(s, axis=1, keepdims=True))
        alpha = jnp.exp2(m_prev - m_new)
        pv = values(weights(s, m_new), j, tq)
        acc_scr[qi] = jnp.tile(alpha, (1, acc_scr.shape[2] // LANES)) * acc_scr[qi] + pv
        m_scr[qi] = m_new

    def normalise(qi):
        acc = acc_scr[qi]
        o = acc[:, :vd] / acc[:, vd:2 * vd]
        lam = lam_ref[0]
        o = jnp.concatenate([o[:h] - lam * o[h:tq], o[tq:tq + h] - lam * o[tq + h:]], axis=0)
        o = o * lax.rsqrt(jnp.mean(o * o, axis=-1, keepdims=True) + RMS_EPS)
        o_ref[pl.ds(_tile_start(qi, tq), tq), :] = (
            (o * g_ref[...]) * (1.0 - lambda_init)).astype(o_ref.dtype)

    def successor(qi, j):
        last = j + 1 == qi
        return (jnp.where(last, jnp.minimum(qi + 1, nq - 1), qi), jnp.where(last, 0, j + 1))

    def full_group(_, carry):
        qi, j = carry
        for g in range(group):
            nqi, nj = successor(qi, j)
            full_scores(bufs[(g + 1) % 2], nqi, nj)
            next_block(bufs[g % 2], qi, j)
            qi, j = nqi, nj
        return qi, j

    diagonal_scores(bufs[0], 0)
    for qi in range(nq):
        if qi + 1 < nq:
            diagonal_scores(bufs[(qi + 1) % 2], qi + 1)
        else:
            full_scores(bufs[(qi + 1) % 2], 1, 0)
        first_block(bufs[qi % 2], qi)
    lax.fori_loop(0, nq * (nq - 1) // 2 // group, full_group, (jnp.int32(1), jnp.int32(0)))
    for qi in range(nq):
        normalise(qi)


def _attention(lam, q, k, v, subln_g, tq, lambda_init):
    bsz, seq, w = q.shape
    hd = 2 * ATT_HEAD_DIM
    nq = seq // tq
    n_full = nq * (nq - 1) // 2
    assert nq % 2 == 0 and n_full % 2 == 0, "score buffers alternate, so both phases need even block counts"
    group = _attn_group(n_full)
    head = pl.BlockSpec((None, seq, hd), lambda b, h: (b, 0, h))
    return pl.pallas_call(
        functools.partial(_attn_kernel, tq=tq, nq=nq, group=group, lambda_init=lambda_init),
        out_shape=jax.ShapeDtypeStruct((bsz, seq, w), BF16),
        grid=(bsz, ATT_HEADS),
        in_specs=[pl.BlockSpec(memory_space=pltpu.SMEM), head, head, head,
                  pl.BlockSpec((1, ATT_V_DIM), lambda b, h: (0, 0))],
        out_specs=head,
        scratch_shapes=[pltpu.VMEM((seq, MXU_WIDTH), BF16),
                        pltpu.VMEM((2 * tq, tq), F32), pltpu.VMEM((2 * tq, tq), F32),
                        pltpu.VMEM((nq, 2 * tq, LANES), F32),
                        pltpu.VMEM((nq, 2 * tq, MXU_WIDTH), F32)],
        compiler_params=_params("parallel", "parallel"),
        name="attn",
    )(lam, q, k, v, subln_g)


def _ffn_chunks(dff):
    step = FFN_CHUNK_TILES * MXU_WIDTH
    return [(c, min(c + step, dff)) for c in range(0, dff, step)]


def _mix_kernel(x_ref, ys_ref, ya_ref, gs_ref, ga_ref, mod_ref, wba_ref, wout_ref, g1_ref, b1_ref,
                wup_ref, cw_ref, cb_ref, wdn_ref, g2_ref, b2_ref, o_ref, carry_scr, *, d, dff):
    @pl.when(pl.program_id(1) == 0)
    def _():
        carry_scr[...] = jnp.zeros_like(carry_scr)

    for rows in _row_groups(x_ref.shape[0], MIX_ROW_GROUP):
        n = rows.stop - rows.start
        att = jnp.dot(ya_ref[rows, :], wba_ref[...], preferred_element_type=F32)
        merged = (gs_ref[rows, :].astype(F32) * ys_ref[rows, :].astype(F32)
                  + ga_ref[rows, :].astype(F32) * att)
        out = jnp.dot(merged.astype(BF16), wout_ref[...], preferred_element_type=F32)
        x1 = (_layernorm(DEEPNORM_ALPHA * x_ref[rows, :] + mod_ref[:, 2 * d:3 * d] * out) * g1_ref[...]
              + b1_ref[...])

        h = (_layernorm(x1) * (1.0 + mod_ref[:, 4 * d:5 * d]) + mod_ref[:, 3 * d:4 * d]).astype(BF16)
        f = jnp.zeros((n, d), F32)
        for c0, c1 in _ffn_chunks(dff):
            a = jnp.dot(h, wup_ref[:, c0:c1], preferred_element_type=F32)
            val = jnp.dot(h, wup_ref[:, dff + c0:dff + c1], preferred_element_type=F32)
            row = lax.broadcasted_iota(jnp.int32, a.shape, 0)
            prev = carry_scr[:, c0:c1]
            p1 = jnp.broadcast_to(prev[7:8], a.shape)
            p2 = jnp.broadcast_to(prev[6:7], a.shape)
            a1 = jnp.where(row == 0, p1, pltpu.roll(a, 1, 0))
            a2 = jnp.where(row == 0, p2, jnp.where(row == 1, p1, pltpu.roll(a, 2, 0)))
            carry_scr[:, c0:c1] = a[n - 8:n]
            ac = cw_ref[0:1, c0:c1] * a2 + cw_ref[1:2, c0:c1] * a1 + cw_ref[2:3, c0:c1] * a + cb_ref[:, c0:c1]
            gl = (jax.nn.silu(ac) * val).astype(BF16)
            f = f + jnp.dot(gl, wdn_ref[c0:c1, :], preferred_element_type=F32)
        o_ref[rows, :] = (_layernorm(DEEPNORM_ALPHA * x1 + mod_ref[:, 5 * d:6 * d] * f) * g2_ref[...]
                          + b2_ref[...])


def _mix(x, ys2d, ya, gs, ga, mod3, w_ba, w_out, ln1_g, ln1_b, w_up, conv_w, conv_b, w_down, ln2_g, ln2_b, tm):
    bsz, seq, d = x.shape
    dff = w_down.shape[0]
    row = pl.BlockSpec((None, tm, d), lambda b, t: (b, t, 0))
    consts = (w_ba, w_out, ln1_g, ln1_b, w_up, conv_w, conv_b, w_down, ln2_g, ln2_b)
    return pl.pallas_call(
        functools.partial(_mix_kernel, d=d, dff=dff),
        out_shape=jax.ShapeDtypeStruct((bsz, seq, d), F32),
        grid=(bsz, seq // tm),
        in_specs=[row, pl.BlockSpec((tm, d), lambda b, t: (t, b)), row, row, row,
                  pl.BlockSpec((None, 1, mod3.shape[2]), lambda b, t: (b, 0, 0))]
                 + [_const_spec(w.shape) for w in consts],
        out_specs=row,
        scratch_shapes=[pltpu.VMEM((8, dff), F32)],
        compiler_params=_params("parallel", "arbitrary"),
        name="mix",
    )(x, ys2d, ya, gs, ga, mod3, *consts)


def kernel(x, c, w_mod, b_mod, w_in, ssm_lambda_re, ssm_lambda_im, ssm_log_dt, ssm_b_re, ssm_b_im, ssm_c_re, ssm_c_im, ssm_d, ssm_w_glu, ssm_b_glu, att_lambda_q1, att_lambda_k1, att_lambda_q2, att_lambda_k2, att_subln_g, w_branch_ssm, w_branch_att, w_out, ln1_g, ln1_b, w_up, conv_w, conv_b, w_down, ln2_g, ln2_b):
    bsz, seq, d = x.shape
    depth = w_mod.shape[0]
    assert depth == DEPTH
    tm = min(ROW_TILE, seq)
    ts = min(SCAN_STEPS, seq)
    tq = min(ATT_TILE, seq)
    for l in range(depth):
        mod3 = _mod(c, w_mod[l].astype(BF16), b_mod[l][None, :])[:, None, :]
        u2d, q, k, v, gs, ga = _inproj(x, mod3, w_in[l].astype(BF16), tm)

        bblk, cblk, a_re, a_im = _ssm_tables(ssm_lambda_re[l], ssm_lambda_im[l], ssm_log_dt[l],
                                             ssm_b_re[l], ssm_b_im[l], ssm_c_re[l], ssm_c_im[l])
        ys_t = _ssm(u2d.reshape(seq, bsz, d), bblk, cblk, a_re, a_im, ssm_d[l][None, :],
                    ssm_w_glu[l].astype(BF16), ssm_b_glu[l][None, :], w_branch_ssm[l].astype(BF16), ts)

        lam = (jnp.exp(jnp.sum(att_lambda_q1[l] * att_lambda_k1[l]))
               - jnp.exp(jnp.sum(att_lambda_q2[l] * att_lambda_k2[l])) + _lambda_init(l))
        ya = _attention(lam.reshape(1).astype(F32), q, k, v, att_subln_g[l][None, :], tq,
                        _lambda_init(l))

        x = _mix(x, ys_t.reshape(seq, bsz * d), ya, gs, ga, mod3, w_branch_att[l].astype(BF16),
                 w_out[l].astype(BF16), ln1_g[l][None, :], ln1_b[l][None, :],
                 w_up[l].astype(BF16), conv_w[l].reshape(CONV_W, -1), conv_b[l][None, :],
                 w_down[l].astype(BF16), ln2_g[l][None, :], ln2_b[l][None, :], tm)
    return x
```

```python
import functools
import math

import jax
import jax.numpy as jnp
from jax import lax
from jax.experimental import pallas as pl
from jax.experimental.pallas import tpu as pltpu

F32 = jnp.float32
BF16 = jnp.bfloat16

CHUNK = 64
ATT_HEADS = 8
ATT_HEAD_DIM = 64
ATT_V_DIM = 2 * ATT_HEAD_DIM
SSM_GROUP = 16
SSM_STATE = 64
CONV_W = 3
DEPTH = 1
DEEPNORM_ALPHA = (2.0 * DEPTH) ** 0.25
LN_EPS = 1e-5
RMS_EPS = 1e-5

def _lambda_init(layer):
    return 0.8 - 0.6 * math.exp(-0.3 * layer)


LANES = 128
MXU_WIDTH = 256
VMEM_LIMIT_BYTES = 56 * 1024 * 1024

GROUP_TILE = 256
SCAN_LANES = 256
FFN_CHUNK_TILES = 6
INPROJ_ROW_GROUP = 256
MIX_ROW_GROUP = 512
ROW_TILE = 512
SCAN_STEPS = 32
ATT_TILE = 512
ATT_GROUP_MAX = 14
NEG_BIG = -1e30


def _layernorm(x):
    mu = jnp.mean(x, axis=-1, keepdims=True)
    xc = x - mu
    var = jnp.mean(xc * xc, axis=-1, keepdims=True)
    return xc * lax.rsqrt(var + LN_EPS)


def _row_groups(tm, group):
    g = min(group, tm)
    return [slice(r, r + g) for r in range(0, tm, g)]


def _const_spec(shape):
    nd = len(shape)
    return pl.BlockSpec(shape, lambda *_: (0,) * nd, pipeline_mode=pl.Buffered(1))


def _params(*sem):
    return pltpu.CompilerParams(dimension_semantics=sem, vmem_limit_bytes=VMEM_LIMIT_BYTES)


def _mod_kernel(c_ref, w_ref, b_ref, o_ref):
    cond = jax.nn.silu(c_ref[...]).astype(BF16)
    o_ref[...] = jnp.dot(cond, w_ref[...], preferred_element_type=F32) + b_ref[...]


def _mod(c, w_mod, b_mod):
    bsz, d = c.shape
    n = w_mod.shape[1]
    tn = d
    return pl.pallas_call(
        _mod_kernel,
        out_shape=jax.ShapeDtypeStruct((bsz, n), F32),
        grid=(n // tn,),
        in_specs=[pl.BlockSpec((bsz, d), lambda j: (0, 0)),
                  pl.BlockSpec((d, tn), lambda j: (0, j)),
                  pl.BlockSpec((1, tn), lambda j: (0, j))],
        out_specs=pl.BlockSpec((bsz, tn), lambda j: (0, j)),
        compiler_params=_params("arbitrary"),
        name="mod",
    )(c, w_mod, b_mod)


def _inproj_kernel(x_ref, mod_ref, w_ref, u_ref, q_ref, k_ref, v_ref, gs_ref, ga_ref, *, d, q_scale):
    for rows in _row_groups(x_ref.shape[0], INPROJ_ROW_GROUP):
        h = _layernorm(x_ref[rows, :])
        h = (h * (1.0 + mod_ref[:, d:2 * d]) + mod_ref[:, 0:d]).astype(BF16)

        def proj(n, h=h):
            return jnp.dot(h, w_ref[:, n * d:(n + 1) * d], preferred_element_type=F32)

        u_ref[rows, :] = proj(0).astype(BF16)
        q_ref[rows, :] = (proj(1) * q_scale).astype(BF16)
        k_ref[rows, :] = proj(2).astype(BF16)
        v_ref[rows, :] = proj(3).astype(BF16)
        gs_ref[rows, :] = jax.nn.sigmoid(proj(4)).astype(BF16)
        ga_ref[rows, :] = jax.nn.sigmoid(proj(5)).astype(BF16)


def _inproj(x, mod3, w_in, tm):
    bsz, seq, d = x.shape
    row = pl.BlockSpec((None, tm, d), lambda b, t: (b, t, 0))
    out_row = jax.ShapeDtypeStruct((bsz, seq, d), BF16)
    return pl.pallas_call(
        functools.partial(_inproj_kernel, d=d, q_scale=ATT_HEAD_DIM ** -0.5 * math.log2(math.e)),
        out_shape=(jax.ShapeDtypeStruct((seq, bsz * d), BF16),) + (out_row,) * 5,
        grid=(bsz, seq // tm),
        in_specs=[row,
                  pl.BlockSpec((None, 1, mod3.shape[2]), lambda b, t: (b, 0, 0)),
                  _const_spec(w_in.shape)],
        out_specs=(pl.BlockSpec((tm, d), lambda b, t: (t, b)),) + (row,) * 5,
        compiler_params=_params("parallel", "parallel"),
        name="inproj",
    )(x, mod3, w_in)


def _ssm_kernel(u_ref, bblk_ref, cblk_ref, are_ref, aim_ref, d_ref,
                out_ref, state_scr, bu_scr, sbf_scr, y_scr, *, ts, bsz, d):
    rows = ts * bsz
    n_tiles = d // GROUP_TILE
    st = GROUP_TILE * SSM_STATE // SSM_GROUP

    @pl.when(pl.program_id(0) == 0)
    def _():
        state_scr[...] = jnp.zeros_like(state_scr)

    u = u_ref[...].reshape(rows, d)
    for j in range(n_tiles):
        cols = slice(j * GROUP_TILE, (j + 1) * GROUP_TILE)
        buf = j % 2
        bu_scr[buf] = jnp.dot(u[:, cols], bblk_ref[j], preferred_element_type=F32)
        for c in range(st // SCAN_LANES):
            re = slice(c * SCAN_LANES, (c + 1) * SCAN_LANES)
            im = slice(st + c * SCAN_LANES, st + (c + 1) * SCAN_LANES)
            ar = jnp.broadcast_to(are_ref[j, :, re], (bsz, SCAN_LANES))
            ai = jnp.broadcast_to(aim_ref[j, :, re], (bsz, SCAN_LANES))
            sr = state_scr[j, :, re]
            si = state_scr[j, :, im]
            for t in range(ts):
                r = slice(t * bsz, (t + 1) * bsz)
                sr, si = (ar * sr - ai * si + bu_scr[buf, r, re],
                          ar * si + ai * sr + bu_scr[buf, r, im])
                sbf_scr[buf, r, re] = sr.astype(BF16)
                sbf_scr[buf, r, im] = si.astype(BF16)
            state_scr[j, :, re] = sr
            state_scr[j, :, im] = si
        y_scr[:, cols] = (jnp.dot(sbf_scr[buf], cblk_ref[j], preferred_element_type=F32)
                          + d_ref[:, cols] * u[:, cols].astype(F32))

    out_ref[...] = y_scr[...].astype(BF16).reshape(ts, bsz, d)


def _ssm(u_t, bblk, cblk, a_re, a_im, d_skip, ts):
    seq, bsz, d = u_t.shape
    rows = ts * bsz
    st2 = bblk.shape[2]
    blk = pl.BlockSpec((ts, bsz, d), lambda i: (i, 0, 0))
    return pl.pallas_call(
        functools.partial(_ssm_kernel, ts=ts, bsz=bsz, d=d),
        out_shape=jax.ShapeDtypeStruct((seq, bsz, d), BF16),
        grid=(seq // ts,),
        in_specs=[blk, _const_spec(bblk.shape), _const_spec(cblk.shape), _const_spec(a_re.shape),
                  _const_spec(a_im.shape), _const_spec(d_skip.shape)],
        out_specs=blk,
        scratch_shapes=[pltpu.VMEM((bblk.shape[0], bsz, st2), F32),
                        pltpu.VMEM((2, rows, st2), F32),
                        pltpu.VMEM((2, rows, st2), BF16),
                        pltpu.VMEM((rows, d), F32)],
        compiler_params=_params("arbitrary"),
        name="ssm",
    )(u_t, bblk, cblk, a_re, a_im, d_skip)


def _ssm_tables(lam_re, lam_im, log_dt, b_re, b_im, c_re, c_im):
    g, p = lam_re.shape
    gpt = GROUP_TILE // SSM_GROUP
    nt = g // gpt
    dt = jnp.exp(log_dt)[:, None]
    mag = jnp.exp(lam_re * dt)
    ab_re = mag * jnp.cos(lam_im * dt)
    ab_im = mag * jnp.sin(lam_im * dt)
    den = lam_re * lam_re + lam_im * lam_im
    nr = ab_re - 1.0
    ni = ab_im
    coef_re = (nr * lam_re + ni * lam_im) / den
    coef_im = (ni * lam_re - nr * lam_im) / den
    bb_re = coef_re[..., None] * b_re - coef_im[..., None] * b_im
    bb_im = coef_re[..., None] * b_im + coef_im[..., None] * b_re
    eye = jnp.eye(gpt, dtype=F32)

    def in_blk(bb):
        t = bb.reshape(nt, gpt, p, SSM_GROUP)
        return jnp.einsum('jgpc,gh->jgchp', t, eye).reshape(nt, gpt * SSM_GROUP, gpt * p)

    def out_blk(cc):
        t = cc.reshape(nt, gpt, SSM_GROUP, p)
        return jnp.einsum('jgcp,gh->jgphc', t, eye).reshape(nt, gpt * p, gpt * SSM_GROUP)

    bblk = jnp.concatenate([in_blk(bb_re), in_blk(bb_im)], axis=2).astype(BF16)
    cblk = jnp.concatenate([out_blk(c_re), -out_blk(c_im)], axis=1).astype(BF16)
    a_re = ab_re.reshape(nt, 1, gpt * p)
    a_im = ab_im.reshape(nt, 1, gpt * p)
    return bblk, cblk, a_re, a_im


def _tile_start(i, size):
    return i * size if isinstance(i, int) else pl.multiple_of(i * size, size)


def _attn_group(n_full):
    return max(g for g in range(2, ATT_GROUP_MAX + 1, 2) if n_full % g == 0)


def _attn_kernel(lam_ref, q_ref, k_ref, v_ref, g_ref, o_ref, vext_scr, sa_scr, sb_scr, m_scr, acc_scr,
                 *, tq, nq, group, lambda_init):
    vd = ATT_V_DIM
    h = tq // 2
    vext_scr[:, :vd] = v_ref[...]
    vext_scr[:, vd:] = jnp.ones((vext_scr.shape[0], vext_scr.shape[1] - vd), vext_scr.dtype)
    lane = lax.broadcasted_iota(jnp.int32, (h, q_ref.shape[1]), 1)
    bufs = (sa_scr, sb_scr)
    nt = (((1,), (1,)), ((), ()))

    def stacked_q(qi):
        r0 = _tile_start(qi, tq)
        parts = []
        for half in (q_ref[pl.ds(r0, h), :], q_ref[pl.ds(r0 + h, h), :]):
            zero = jnp.zeros_like(half)
            parts += [jnp.where(lane < ATT_HEAD_DIM, half, zero), jnp.where(lane >= ATT_HEAD_DIM, half, zero)]
        return jnp.concatenate(parts, axis=0)

    def keys(j, n):
        return k_ref[pl.ds(_tile_start(j, tq), n), :]

    def weights(s, m):
        return jnp.exp2(s - jnp.tile(m, (1, s.shape[1] // LANES))).astype(BF16)

    def values(p, j, n):
        return jnp.dot(p, vext_scr[pl.ds(_tile_start(j, tq), n), :],
                       preferred_element_type=F32)

    def full_scores(buf, qi, j):
        buf[...] = lax.dot_general(stacked_q(qi), keys(j, tq), nt, preferred_element_type=F32)

    def diagonal_scores(buf, qi):
        qs = stacked_q(qi)
        buf[:tq, :h] = lax.dot_general(qs[:tq], keys(qi, h), nt, preferred_element_type=F32)
        buf[tq:, :] = lax.dot_general(qs[tq:], keys(qi, tq), nt, preferred_element_type=F32)

    def first_block(buf, qi):
        row = lax.broadcasted_iota(jnp.int32, (tq, 1), 0) % h
        for rows, n, pos in ((slice(0, tq), h, row), (slice(tq, 2 * tq), tq, row + h)):
            s = buf[rows, :n]
            col = lax.broadcasted_iota(jnp.int32, s.shape, 1)
            s = jnp.where(col < (pos // CHUNK + 1) * CHUNK, s, NEG_BIG)
            m = jnp.broadcast_to(jnp.max(s, axis=1, keepdims=True), (tq, LANES))
            acc_scr[qi, rows, :] = values(weights(s, m), qi, n)
            m_scr[qi, rows, :] = m

    def next_block(buf, qi, j):
        s = buf[...]
        m_prev = m_scr[qi]
        m_new = jnp.maximum(m_prev, jnp.max(s, axis=1, keepdims=True))
        alpha = jnp.exp2(m_prev - m_new)
        pv = values(weights(s, m_new), j, tq)
        acc_scr[qi] = jnp.tile(alpha, (1, acc_scr.shape[2] // LANES)) * acc_scr[qi] + pv
        m_scr[qi] = m_new

    def normalise(qi):
        acc = acc_scr[qi]
        o = acc[:, :vd] / acc[:, vd:2 * vd]
        lam = lam_ref[0]
        o = jnp.concatenate([o[:h] - lam * o[h:tq], o[tq:tq + h] - lam * o[tq + h:]], axis=0)
        o = o * lax.rsqrt(jnp.mean(o * o, axis=-1, keepdims=True) + RMS_EPS)
        o_ref[pl.ds(_tile_start(qi, tq), tq), :] = (
            (o * g_ref[...]) * (1.0 - lambda_init)).astype(o_ref.dtype)

    def successor(qi, j):
        last = j + 1 == qi
        return (jnp.where(last, jnp.minimum(qi + 1, nq - 1), qi), jnp.where(last, 0, j + 1))

    def full_group(_, carry):
        qi, j = carry
        for g in range(group):
            nqi, nj = successor(qi, j)
            full_scores(bufs[(g + 1) % 2], nqi, nj)
            next_block(bufs[g % 2], qi, j)
            qi, j = nqi, nj
        return qi, j

    diagonal_scores(bufs[0], 0)
    for qi in range(nq):
        if qi + 1 < nq:
            diagonal_scores(bufs[(qi + 1) % 2], qi + 1)
        else:
            full_scores(bufs[(qi + 1) % 2], 1, 0)
        first_block(bufs[qi % 2], qi)
    lax.fori_loop(0, nq * (nq - 1) // 2 // group, full_group, (jnp.int32(1), jnp.int32(0)))
    for qi in range(nq):
        normalise(qi)


def _attention(lam, q, k, v, subln_g, tq, lambda_init):
    bsz, seq, w = q.shape
    hd = 2 * ATT_HEAD_DIM
    nq = seq // tq
    n_full = nq * (nq - 1) // 2
    assert nq % 2 == 0 and n_full % 2 == 0, "score buffers alternate, so both phases need even block counts"
    group = _attn_group(n_full)
    head = pl.BlockSpec((None, seq, hd), lambda b, h: (b, 0, h))
    return pl.pallas_call(
        functools.partial(_attn_kernel, tq=tq, nq=nq, group=group, lambda_init=lambda_init),
        out_shape=jax.ShapeDtypeStruct((bsz, seq, w), BF16),
        grid=(bsz, ATT_HEADS),
        in_specs=[pl.BlockSpec(memory_space=pltpu.SMEM), head, head, head,
                  pl.BlockSpec((1, ATT_V_DIM), lambda b, h: (0, 0))],
        out_specs=head,
        scratch_shapes=[pltpu.VMEM((seq, MXU_WIDTH), BF16),
                        pltpu.VMEM((2 * tq, tq), F32), pltpu.VMEM((2 * tq, tq), F32),
                        pltpu.VMEM((nq, 2 * tq, LANES), F32),
                        pltpu.VMEM((nq, 2 * tq, MXU_WIDTH), F32)],
        compiler_params=_params("parallel", "parallel"),
        name="attn",
    )(lam, q, k, v, subln_g)


def _ffn_chunks(dff):
    step = FFN_CHUNK_TILES * MXU_WIDTH
    return [(c, min(c + step, dff)) for c in range(0, dff, step)]


def _mix_kernel(x_ref, ys_ref, ya_ref, gs_ref, ga_ref, mod_ref, wglu_ref, bglu_ref, wbs_ref,
                wba_ref, wout_ref, g1_ref, b1_ref,
                wup_ref, cw_ref, cb_ref, wdn_ref, g2_ref, b2_ref, o_ref, carry_scr, *, d, dff):
    @pl.when(pl.program_id(1) == 0)
    def _():
        carry_scr[...] = jnp.zeros_like(carry_scr)

    for rows in _row_groups(x_ref.shape[0], MIX_ROW_GROUP):
        n = rows.stop - rows.start
        z = jax.nn.gelu(ys_ref[rows, :].astype(F32))
        gate = jax.nn.sigmoid(jnp.dot(z.astype(BF16), wglu_ref[...], preferred_element_type=F32)
                              + bglu_ref[...])
        ssm = jnp.dot((z * gate).astype(BF16), wbs_ref[...], preferred_element_type=F32)
        att = jnp.dot(ya_ref[rows, :], wba_ref[...], preferred_element_type=F32)
        merged = gs_ref[rows, :].astype(F32) * ssm + ga_ref[rows, :].astype(F32) * att
        out = jnp.dot(merged.astype(BF16), wout_ref[...], preferred_element_type=F32)
        x1 = (_layernorm(DEEPNORM_ALPHA * x_ref[rows, :] + mod_ref[:, 2 * d:3 * d] * out) * g1_ref[...]
              + b1_ref[...])

        h = (_layernorm(x1) * (1.0 + mod_ref[:, 4 * d:5 * d]) + mod_ref[:, 3 * d:4 * d]).astype(BF16)
        f = jnp.zeros((n, d), F32)
        for c0, c1 in _ffn_chunks(dff):
            a = jnp.dot(h, wup_ref[:, c0:c1], preferred_element_type=F32)
            val = jnp.dot(h, wup_ref[:, dff + c0:dff + c1], preferred_element_type=F32)
            row = lax.broadcasted_iota(jnp.int32, a.shape, 0)
            prev = carry_scr[:, c0:c1]
            p1 = jnp.broadcast_to(prev[7:8], a.shape)
            p2 = jnp.broadcast_to(prev[6:7], a.shape)
            a1 = jnp.where(row == 0, p1, pltpu.roll(a, 1, 0))
            a2 = jnp.where(row == 0, p2, jnp.where(row == 1, p1, pltpu.roll(a, 2, 0)))
            carry_scr[:, c0:c1] = a[n - 8:n]
            ac = cw_ref[0:1, c0:c1] * a2 + cw_ref[1:2, c0:c1] * a1 + cw_ref[2:3, c0:c1] * a + cb_ref[:, c0:c1]
            gl = (jax.nn.silu(ac) * val).astype(BF16)
            f = f + jnp.dot(gl, wdn_ref[c0:c1, :], preferred_element_type=F32)
        o_ref[rows, :] = (_layernorm(DEEPNORM_ALPHA * x1 + mod_ref[:, 5 * d:6 * d] * f) * g2_ref[...]
                          + b2_ref[...])


def _mix(x, ys2d, ya, gs, ga, mod3, w_glu, b_glu, w_bs, w_ba, w_out, ln1_g, ln1_b, w_up, conv_w, conv_b,
         w_down, ln2_g, ln2_b, tm):
    bsz, seq, d = x.shape
    dff = w_down.shape[0]
    row = pl.BlockSpec((None, tm, d), lambda b, t: (b, t, 0))
    consts = (w_glu, b_glu, w_bs, w_ba, w_out, ln1_g, ln1_b, w_up, conv_w, conv_b, w_down, ln2_g, ln2_b)
    return pl.pallas_call(
        functools.partial(_mix_kernel, d=d, dff=dff),
        out_shape=jax.ShapeDtypeStruct((bsz, seq, d), F32),
        grid=(bsz, seq // tm),
        in_specs=[row, pl.BlockSpec((tm, d), lambda b, t: (t, b)), row, row, row,
                  pl.BlockSpec((None, 1, mod3.shape[2]), lambda b, t: (b, 0, 0))]
                 + [_const_spec(w.shape) for w in consts],
        out_specs=row,
        scratch_shapes=[pltpu.VMEM((8, dff), F32)],
        compiler_params=_params("parallel", "arbitrary"),
        name="mix",
    )(x, ys2d, ya, gs, ga, mod3, *consts)


def kernel(x, c, w_mod, b_mod, w_in, ssm_lambda_re, ssm_lambda_im, ssm_log_dt, ssm_b_re, ssm_b_im, ssm_c_re, ssm_c_im, ssm_d, ssm_w_glu, ssm_b_glu, att_lambda_q1, att_lambda_k1, att_lambda_q2, att_lambda_k2, att_subln_g, w_branch_ssm, w_branch_att, w_out, ln1_g, ln1_b, w_up, conv_w, conv_b, w_down, ln2_g, ln2_b):
    bsz, seq, d = x.shape
    depth = w_mod.shape[0]
    assert depth == DEPTH
    tm = min(ROW_TILE, seq)
    ts = min(SCAN_STEPS, seq)
    tq = min(ATT_TILE, seq)
    for l in range(depth):
        mod3 = _mod(c, w_mod[l].astype(BF16), b_mod[l][None, :])[:, None, :]
        u2d, q, k, v, gs, ga = _inproj(x, mod3, w_in[l].astype(BF16), tm)

        bblk, cblk, a_re, a_im = _ssm_tables(ssm_lambda_re[l], ssm_lambda_im[l], ssm_log_dt[l],
                                             ssm_b_re[l], ssm_b_im[l], ssm_c_re[l], ssm_c_im[l])
        ys_t = _ssm(u2d.reshape(seq, bsz, d), bblk, cblk, a_re, a_im, ssm_d[l][None, :], ts)

        lam = (jnp.exp(jnp.sum(att_lambda_q1[l] * att_lambda_k1[l]))
               - jnp.exp(jnp.sum(att_lambda_q2[l] * att_lambda_k2[l])) + _lambda_init(l))
        ya = _attention(lam.reshape(1).astype(F32), q, k, v, att_subln_g[l][None, :], tq,
                        _lambda_init(l))

        x = _mix(x, ys_t.reshape(seq, bsz * d), ya, gs, ga, mod3, ssm_w_glu[l].astype(BF16),
                 ssm_b_glu[l][None, :], w_branch_ssm[l].astype(BF16), w_branch_att[l].astype(BF16),
                 w_out[l].astype(BF16), ln1_g[l][None, :], ln1_b[l][None, :],
                 w_up[l].astype(BF16), conv_w[l].reshape(CONV_W, -1), conv_b[l][None, :],
                 w_down[l].astype(BF16), ln2_g[l][None, :], ln2_b[l][None, :], tm)
    return x
```
